```python
import math
import jax, jax.numpy as jnp
from jax import lax
import numpy as np

D_MODEL = 1024
BATCH = 1
SEQ = 16384
DEPTH = 2
DEC_BATCH = 128
DEC_SEQ = 1
PAST_LEN = 16384
PAGE_SIZE = 128

N_META = 16
HEAD_DIM = 64
N_HEADS = D_MODEL // HEAD_DIM
N_KV_HEADS = N_HEADS // 4
GROUP = N_HEADS // N_KV_HEADS
D_ATT = N_HEADS * HEAD_DIM
D_KV = N_KV_HEADS * HEAD_DIM
WINDOW = 128
BLOCK = 128
D_CONV = D_MODEL
CONV_K = 31
D_FF = 2816
FFN_K = 3
D_IN = 2 * D_CONV + D_ATT + 2 * D_KV + 2 * D_MODEL
EPS = 1e-6
ATT_SCALE = 1.0 / math.sqrt(HEAD_DIM)
NEG = -1e30

kernel_name = 'hybrid_conformer_swa_sink_convffn_step'


def rmsnorm(x, g):
    xf = x.astype(jnp.float32)
    y = xf * lax.rsqrt(jnp.mean(xf * xf, axis=-1, keepdims=True) + EPS)
    return (y * g.astype(jnp.float32)).astype(x.dtype)


def layernorm(x, g, b):
    xf = x.astype(jnp.float32)
    mu = jnp.mean(xf, axis=-1, keepdims=True)
    var = jnp.mean(jnp.square(xf - mu), axis=-1, keepdims=True)
    y = (xf - mu) * lax.rsqrt(var + EPS)
    return (y * g.astype(jnp.float32) + b.astype(jnp.float32)).astype(x.dtype)


def causal_dwconv(x_hist, w, b):
    C = x_hist.shape[-1]
    out = lax.conv_general_dilated(x_hist, w[:, None, :], window_strides=(1,), padding='VALID',
                                   dimension_numbers=('NWC', 'WIO', 'NWC'), feature_group_count=C)
    return out + b


def split_in(h):
    offs = [D_CONV, 2 * D_CONV, 2 * D_CONV + D_ATT, 2 * D_CONV + D_ATT + D_KV,
            2 * D_CONV + D_ATT + 2 * D_KV, 2 * D_CONV + D_ATT + 2 * D_KV + D_MODEL]
    return jnp.split(h, offs, axis=-1)


def sink_softmax(s, sink, mask):
    s = jnp.where(mask, s, NEG)
    m = jnp.maximum(jnp.max(s, axis=-1, keepdims=True), sink)
    e = jnp.exp(s - m)
    return e / (jnp.sum(e, axis=-1, keepdims=True) + jnp.exp(sink - m))


def swa_banded(q, k, v, sinks):
    B, L = q.shape[:2]
    pad = (-L) % BLOCK
    nb = (L + pad) // BLOCK
    pw = ((0, 0), (pad, 0), (0, 0), (0, 0))
    qb = jnp.pad(q, pw).reshape(B, nb, BLOCK, N_KV_HEADS, GROUP, HEAD_DIM)
    kb = jnp.pad(k, pw).reshape(B, nb, BLOCK, N_KV_HEADS, HEAD_DIM)
    vb = jnp.pad(v, pw).reshape(B, nb, BLOCK, N_KV_HEADS, HEAD_DIM)
    pb = ((0, 0), (1, 0), (0, 0), (0, 0), (0, 0))
    kk = jnp.concatenate([jnp.pad(kb, pb)[:, :-1], kb], axis=2)
    vv = jnp.concatenate([jnp.pad(vb, pb)[:, :-1], vb], axis=2)
    s = jnp.einsum('bnqkgd,bnskd->bnkgqs', qb, kk, preferred_element_type=jnp.float32) * ATT_SCALE
    blk = jnp.arange(nb)[:, None, None] * BLOCK
    qi = blk + jnp.arange(BLOCK)[None, :, None]
    ki = blk - BLOCK + jnp.arange(2 * BLOCK)[None, None, :]
    rel = qi - ki
    mask = ((rel >= 0) & (rel <= WINDOW) & (ki >= pad))[None, :, None, None]
    sink = sinks.astype(jnp.float32).reshape(1, 1, N_KV_HEADS, GROUP, 1, 1)
    pr = sink_softmax(s, sink, mask)
    o = jnp.einsum('bnkgqs,bnskd->bnqkgd', pr.astype(v.dtype), vv)
    return o.reshape(B, nb * BLOCK, D_ATT)[:, pad:]


def swa_decode(q, k, v, k_buf, v_buf, sinks):
    B, S = q.shape[:2]
    kk = jnp.concatenate([k_buf, k], axis=1)
    vv = jnp.concatenate([v_buf, v], axis=1)
    qg = q.reshape(B, S, N_KV_HEADS, GROUP, HEAD_DIM)
    s = jnp.einsum('bqkgd,bskd->bkgqs', qg, kk, preferred_element_type=jnp.float32) * ATT_SCALE
    qpos = PAST_LEN + jnp.arange(S)
    kpos = PAST_LEN - WINDOW + jnp.arange(WINDOW + S)
    rel = qpos[:, None] - kpos[None, :]
    mask = ((rel >= 0) & (rel <= WINDOW))[None, None, None]
    sink = sinks.astype(jnp.float32).reshape(1, N_KV_HEADS, GROUP, 1, 1)
    pr = sink_softmax(s, sink, mask)
    o = jnp.einsum('bkgqs,bskd->bqkgd', pr.astype(v.dtype), vv).reshape(B, S, D_ATT)
    return o, kk[:, -WINDOW:], vv[:, -WINDOW:]


def token_mixer(x, p, conv_hist, kv_buf):
    B, T, _ = x.shape
    xn = rmsnorm(x, p['norm_mix'])
    ua, ub, q, k, v, g_conv, g_att = split_in(xn @ p['w_in'])
    u = ua * jax.nn.sigmoid(ub)
    u_hist = jnp.concatenate([conv_hist, u], axis=1)
    c = causal_dwconv(u_hist, p['conv_dw'], p['conv_db'])
    c = jax.nn.silu(layernorm(c, p['conv_ln_g'], p['conv_ln_b'])) @ p['w_conv_pw']
    q = q.reshape(B, T, N_HEADS, HEAD_DIM)
    k = k.reshape(B, T, N_KV_HEADS, HEAD_DIM)
    v = v.reshape(B, T, N_KV_HEADS, HEAD_DIM)
    if kv_buf is None:
        a = swa_banded(q, k, v, p['attn_sinks'])
        new_k, new_v = k[:, -WINDOW:], v[:, -WINDOW:]
    else:
        a, new_k, new_v = swa_decode(q, k, v, kv_buf[0], kv_buf[1], p['attn_sinks'])
    a = a @ p['w_attn_o']
    m = jax.nn.sigmoid(g_conv) * c + jax.nn.sigmoid(g_att) * a
    return m @ p['w_out'], new_k, new_v, u_hist[:, -(CONV_K - 1):]


def conv_ffn(x, p, ffn_hist):
    xn = rmsnorm(x, p['norm_ffn'])
    h, g = jnp.split(xn @ p['w_ffn_up'], [D_FF], axis=-1)
    h_hist = jnp.concatenate([ffn_hist, h], axis=1)
    c = causal_dwconv(h_hist, p['ffn_dw'], p['ffn_db'])
    return (jax.nn.gelu(c) * g) @ p['w_ffn_down'], h_hist[:, -(FFN_K - 1):]


def block(x, p, conv_hist, ffn_hist, kv_buf):
    dx, nk, nv, nc = token_mixer(x, p, conv_hist, kv_buf)
    x = x + dx
    dx, nf = conv_ffn(x, p, ffn_hist)
    return x + dx, nk, nv, nc, nf


def setup_inputs(seed: int = 0) -> dict:
    key = jax.random.key(seed)
    ks = jax.random.split(key, 24)
    f32 = jnp.float32
    nrm = lambda k, shape, s: jax.random.normal(k, shape, f32) * s
    return {
        'x_prompt': nrm(ks[0], (BATCH, SEQ, D_MODEL), 1.0),
        'x_sample': nrm(ks[1], (DEC_BATCH, DEC_SEQ, D_MODEL), 1.0),
        'cache_swa_k': nrm(ks[2], (DEPTH, DEC_BATCH, WINDOW, N_KV_HEADS, HEAD_DIM), 1.0),
        'cache_swa_v': nrm(ks[3], (DEPTH, DEC_BATCH, WINDOW, N_KV_HEADS, HEAD_DIM), 1.0),
        'state_conv': nrm(ks[4], (DEPTH, DEC_BATCH, CONV_K - 1, D_CONV), 0.5),
        'state_ffn_conv': nrm(ks[5], (DEPTH, DEC_BATCH, FFN_K - 1, D_FF), 1.0),
        'meta_tokens': nrm(ks[6], (N_META, D_MODEL), 1.0),
        'norm_mix': 1.0 + nrm(ks[7], (DEPTH, D_MODEL), 0.02),
        'w_in': nrm(ks[8], (DEPTH, D_MODEL, D_IN), D_MODEL ** -0.5),
        'conv_dw': nrm(ks[9], (DEPTH, CONV_K, D_CONV), CONV_K ** -0.5),
        'conv_db': nrm(ks[10], (DEPTH, D_CONV), 0.02),
        'conv_ln_g': 1.0 + nrm(ks[11], (DEPTH, D_CONV), 0.02),
        'conv_ln_b': nrm(ks[12], (DEPTH, D_CONV), 0.02),
        'w_conv_pw': nrm(ks[13], (DEPTH, D_CONV, D_MODEL), D_CONV ** -0.5),
        'attn_sinks': nrm(ks[14], (DEPTH, N_HEADS), 0.5),
        'w_attn_o': nrm(ks[15], (DEPTH, D_ATT, D_MODEL), D_ATT ** -0.5),
        'w_out': nrm(ks[16], (DEPTH, D_MODEL, D_MODEL), D_MODEL ** -0.5),
        'norm_ffn': 1.0 + nrm(ks[17], (DEPTH, D_MODEL), 0.02),
        'w_ffn_up': nrm(ks[18], (DEPTH, D_MODEL, 2 * D_FF), D_MODEL ** -0.5),
        'ffn_dw': nrm(ks[19], (DEPTH, FFN_K, D_FF), FFN_K ** -0.5),
        'ffn_db': nrm(ks[20], (DEPTH, D_FF), 0.02),
        'w_ffn_down': nrm(ks[21], (DEPTH, D_FF, D_MODEL), D_FF ** -0.5),
        'norm_final': 1.0 + nrm(ks[22], (D_MODEL,), 0.02),
    }


def reference(x_prompt, x_sample, cache_swa_k, cache_swa_v, state_conv, state_ffn_conv,
              meta_tokens, norm_mix, w_in, conv_dw, conv_db, conv_ln_g, conv_ln_b, w_conv_pw,
              attn_sinks, w_attn_o, w_out, norm_ffn, w_ffn_up, ffn_dw, ffn_db, w_ffn_down, norm_final):
    B = x_prompt.shape[0]
    meta = jnp.broadcast_to(meta_tokens[None].astype(x_prompt.dtype), (B, N_META, D_MODEL))
    xp = jnp.concatenate([meta, x_prompt], axis=1)
    xs = x_sample
    kp_l, vp_l, cp_l, fp_l = [], [], [], []
    ks_l, vs_l, cs_l, fs_l = [], [], [], []
    for l in range(DEPTH):
        p = dict(norm_mix=norm_mix[l], w_in=w_in[l], conv_dw=conv_dw[l], conv_db=conv_db[l],
                 conv_ln_g=conv_ln_g[l], conv_ln_b=conv_ln_b[l], w_conv_pw=w_conv_pw[l],
                 attn_sinks=attn_sinks[l], w_attn_o=w_attn_o[l], w_out=w_out[l], norm_ffn=norm_ffn[l],
                 w_ffn_up=w_ffn_up[l], ffn_dw=ffn_dw[l], ffn_db=ffn_db[l], w_ffn_down=w_ffn_down[l])
        zc = jnp.zeros((B, CONV_K - 1, D_CONV), xp.dtype)
        zf = jnp.zeros((B, FFN_K - 1, D_FF), xp.dtype)
        xp, nk, nv, nc, nf = block(xp, p, zc, zf, None)
        kp_l.append(nk); vp_l.append(nv); cp_l.append(nc); fp_l.append(nf)
        xs, nk, nv, nc, nf = block(xs, p, state_conv[l], state_ffn_conv[l], (cache_swa_k[l], cache_swa_v[l]))
        ks_l.append(nk); vs_l.append(nv); cs_l.append(nc); fs_l.append(nf)
    y_prompt = rmsnorm(xp, norm_final)[:, N_META:]
    y_sample = rmsnorm(xs, norm_final)
    return (y_prompt, y_sample,
            jnp.stack(kp_l), jnp.stack(vp_l), jnp.stack(cp_l), jnp.stack(fp_l),
            jnp.stack(ks_l), jnp.stack(vs_l), jnp.stack(cs_l), jnp.stack(fs_l))
```

```python
import functools

import jax
import jax.numpy as jnp
from jax import lax
from jax.experimental import pallas as pl
from jax.experimental.pallas import tpu as pltpu

F32 = jnp.float32
BF16 = jnp.bfloat16

D_MODEL = 1024
N_META = 16
HEAD_DIM = 64
N_KV = 4
GROUP = 4
D_KV = N_KV * HEAD_DIM
WINDOW = 128
CONV_K = 31
D_FF = 2816
FFN_K = 3
EPS = 1e-6
NEG = -1e30
ATT_SCALE = 0.125

LANES = 128
SUBLANES = 8
N_CCH = D_MODEL // LANES
N_FCH = D_FF // LANES

TILE = 512
PAD = TILE - N_META
CONV_HIST = 32
FFN_HIST = 8
ABLK = 128
SB = 8

O_UA, O_UB, O_Q, O_K, O_V, O_GC, O_GA, O_END = 0, 1024, 2048, 3072, 3328, 3584, 4608, 5632

VMEM_LIMIT = 60 * 1024 * 1024


def _sigmoid(x):
    return 1.0 / (1.0 + jnp.exp(-x))


def _gelu_tanh(x):
    return 0.5 * x * (1.0 + jnp.tanh(0.7978845608028654 * (x + 0.044715 * (x * x * x))))


def _rmsnorm(x, g):
    ms = jnp.mean(x * x, axis=-1, keepdims=True)
    return x * lax.rsqrt(ms + EPS) * g


def _layernorm(x, g, b):
    mu = jnp.mean(x, axis=-1, keepdims=True)
    d = x - mu
    var = jnp.mean(d * d, axis=-1, keepdims=True)
    return d * lax.rsqrt(var + EPS) * g + b


def _dot(a, b):
    return jnp.dot(a, b, preferred_element_type=F32)


def _dot_nt(a, b):
    return lax.dot_general(a, b, (((1,), (1,)), ((), ())), preferred_element_type=F32)


def _pmix_kernel(first, *refs):
    if first:
        xf_ref, refs = refs[0], refs[1:]
    (xm_ref, gmix_ref, win_ref, dwb_ref, db_ref, lng_ref, lnb_ref, wpw_ref, sinks_ref,
     wao_ref, wout_ref,
     xo_ref, knew_ref, vnew_ref, unew_ref,
     ubuf, c_s, q_s, k_s, v_s, a_s) = refs
    T = TILE
    i = pl.program_id(0)

    @pl.when(i == 0)
    def _():
        ubuf[:, 0:CONV_HIST, :] = jnp.zeros((N_CCH, CONV_HIST, LANES), F32)
        k_s[:, 0:WINDOW, :] = jnp.zeros((N_KV, WINDOW, D_KV), BF16)
        v_s[:, 0:WINDOW, :] = jnp.zeros((N_KV, WINDOW, D_KV), BF16)

    x = xm_ref[...]
    if first:
        x = jnp.where(i == 0, xf_ref[...], x)
    xn = _rmsnorm(x, gmix_ref[...]).astype(BF16)

    def proj(lo, hi):
        return _dot(xn, win_ref[:, lo:hi])

    rows = i * T + lax.broadcasted_iota(jnp.int32, (T, 1), 0)
    real = rows >= PAD

    u = proj(O_UA, O_UB) * _sigmoid(proj(O_UB, O_Q))
    u = jnp.where(real, u, 0.0)
    for c in range(N_CCH):
        ubuf[c, CONV_HIST:CONV_HIST + T, :] = u[:, c * LANES:(c + 1) * LANES]
    unew_ref[...] = u[T - CONV_HIST:, :]

    for c in range(N_CCH):
        ws = [dwb_ref[c, j * SUBLANES:(j + 1) * SUBLANES, :] for j in range(CONV_K)]
        bias = jnp.broadcast_to(db_ref[:, c * LANES:(c + 1) * LANES], (SUBLANES, LANES))

        def conv_body(r, carry, c=c, ws=ws, bias=bias):
            r0 = r * SUBLANES
            accs = [bias, None, None, None]
            for j in range(CONV_K):
                t = ubuf[c, pl.ds(r0 + (CONV_HIST - (CONV_K - 1)) + j, SUBLANES), :] * ws[j]
                a = j % 4
                accs[a] = t if accs[a] is None else accs[a] + t
            c_s[pl.ds(pl.multiple_of(r0, SUBLANES), SUBLANES), c * LANES:(c + 1) * LANES] = (
                (accs[0] + accs[1]) + (accs[2] + accs[3]))
            return carry

        lax.fori_loop(0, T // SUBLANES, conv_body, 0, unroll=2)

    y = _layernorm(c_s[...], lng_ref[...], lnb_ref[...])
    y = y * _sigmoid(y)
    cbr = _dot(y.astype(BF16), wpw_ref[...])

    q_s[...] = proj(O_Q, O_K).astype(BF16)
    k = proj(O_K, O_V)
    v = proj(O_V, O_GC)
    knew_ref[...] = k[T - WINDOW:, :]
    vnew_ref[...] = v[T - WINDOW:, :]
    lane_kh = lax.broadcasted_iota(jnp.int32, (1, D_KV), 1) // HEAD_DIM
    for kh in range(N_KV):
        sel = lane_kh == kh
        k_s[kh, WINDOW:WINDOW + T, :] = jnp.where(sel, k, 0.0).astype(BF16)
        v_s[kh, WINDOW:WINDOW + T, :] = jnp.where(sel, v, 0.0).astype(BF16)

    qi = lax.broadcasted_iota(jnp.int32, (ABLK, 2 * ABLK), 0)
    kj = lax.broadcasted_iota(jnp.int32, (ABLK, 2 * ABLK), 1)
    band = (kj >= qi) & (kj <= qi + WINDOW)

    def attn_block(b, carry):
        r0 = pl.multiple_of(b * ABLK, ABLK)
        ok = band & (i * T + r0 - WINDOW + kj >= PAD)
        qb = q_s[pl.ds(r0, ABLK), :]
        qall = jnp.concatenate([qb[:, g * D_KV:(g + 1) * D_KV] for g in range(GROUP)], axis=0)
        probs = [[None] * N_KV for _ in range(GROUP)]
        for kh in range(N_KV):
            s = _dot_nt(qall, k_s[kh, pl.ds(r0, 2 * ABLK), :])
            for g in range(GROUP):
                sink = sinks_ref[g * N_KV + kh]
                sh = jnp.where(ok, s[g * ABLK:(g + 1) * ABLK, :], NEG)
                m = jnp.maximum(jnp.max(sh, axis=-1, keepdims=True), sink)
                e = jnp.exp(sh - m)
                l = jnp.sum(e, axis=-1, keepdims=True) + jnp.exp(sink - m)
                probs[g][kh] = (e * (1.0 / l)).astype(BF16)
        vst = jnp.concatenate([v_s[kh, pl.ds(r0, 2 * ABLK), :] for kh in range(N_KV)], axis=0)
        for g in range(GROUP):
            o = _dot(jnp.concatenate(probs[g], axis=1), vst)
            a_s[pl.ds(r0, ABLK), g * D_KV:(g + 1) * D_KV] = o.astype(BF16)
        return carry

    lax.fori_loop(0, T // ABLK, attn_block, 0)
    abr = _dot(a_s[...], wao_ref[...])

    mix = _sigmoid(proj(O_GC, O_GA)) * cbr + _sigmoid(proj(O_GA, O_END)) * abr
    xo_ref[...] = x + _dot(mix.astype(BF16), wout_ref[...])

    for c in range(N_CCH):
        ubuf[c, 0:CONV_HIST, :] = ubuf[c, T:T + CONV_HIST, :]
    k_s[:, 0:WINDOW, :] = k_s[:, T:T + WINDOW, :]
    v_s[:, 0:WINDOW, :] = v_s[:, T:T + WINDOW, :]


def _const_spec(shape):
    zeros = (0,) * len(shape)
    return pl.BlockSpec(shape, lambda i: zeros, pipeline_mode=pl.Buffered(1))


def _pmix_call(first, n_tiles, x_first, x_main, gmix, win, dwb, db, lng, lnb, wpw, sinks, wao, wout):
    T = TILE
    if first:
        x_spec = pl.BlockSpec((T, D_MODEL), lambda i: (jnp.maximum(i - 1, 0), 0))
    else:
        x_spec = pl.BlockSpec((T, D_MODEL), lambda i: (i, 0))
    in_specs = [
        x_spec,
        _const_spec((1, D_MODEL)),
        _const_spec((D_MODEL, O_END)),
        _const_spec((N_CCH, CONV_K * SUBLANES, LANES)),
        _const_spec((1, D_MODEL)),
        _const_spec((1, D_MODEL)),
        _const_spec((1, D_MODEL)),
        _const_spec((D_MODEL, D_MODEL)),
        pl.BlockSpec(memory_space=pltpu.SMEM),
        _const_spec((D_MODEL, D_MODEL)),
        _const_spec((D_MODEL, D_MODEL)),
    ]
    args = [x_main, gmix, win, dwb, db, lng, lnb, wpw, sinks, wao, wout]
    if first:
        in_specs = [_const_spec((T, D_MODEL))] + in_specs
        args = [x_first] + args
    out_shape = (
        jax.ShapeDtypeStruct((n_tiles * T, D_MODEL), F32),
        jax.ShapeDtypeStruct((WINDOW, D_KV), F32),
        jax.ShapeDtypeStruct((WINDOW, D_KV), F32),
        jax.ShapeDtypeStruct((CONV_HIST, D_MODEL), F32),
    )
    out_specs = (
        pl.BlockSpec((T, D_MODEL), lambda i: (i, 0)),
        pl.BlockSpec((WINDOW, D_KV), lambda i: (0, 0)),
        pl.BlockSpec((WINDOW, D_KV), lambda i: (0, 0)),
        pl.BlockSpec((CONV_HIST, D_MODEL), lambda i: (0, 0)),
    )
    scratch = [
        pltpu.VMEM((N_CCH, CONV_HIST + T, LANES), F32),
        pltpu.VMEM((T, D_MODEL), F32),
        pltpu.VMEM((T, D_MODEL), BF16),
        pltpu.VMEM((N_KV, WINDOW + T, D_KV), BF16),
        pltpu.VMEM((N_KV, WINDOW + T, D_KV), BF16),
        pltpu.VMEM((T, D_MODEL), BF16),
    ]
    return pl.pallas_call(
        functools.partial(_pmix_kernel, first),
        grid=(n_tiles,),
        in_specs=in_specs,
        out_specs=out_specs,
        out_shape=out_shape,
        scratch_shapes=scratch,
        compiler_params=pltpu.CompilerParams(
            dimension_semantics=("arbitrary",), vmem_limit_bytes=VMEM_LIMIT),
        name="prompt_mixer",
    )(*args)


def _pffn_kernel(final, *refs):
    if final:
        (xm_ref, gffn_ref, wup_ref, fdw_ref, fdb_ref, wdn_ref, gfin_ref,
         xo_ref, hnew_ref, hbuf) = refs
    else:
        (xm_ref, gffn_ref, wup_ref, fdw_ref, fdb_ref, wdn_ref,
         xo_ref, hnew_ref, hbuf) = refs
    T = TILE
    i = pl.program_id(0)

    @pl.when(i == 0)
    def _():
        hbuf[:, 0:FFN_HIST, :] = jnp.zeros((N_FCH, FFN_HIST, LANES), F32)

    x = xm_ref[...]
    xn = _rmsnorm(x, gffn_ref[...]).astype(BF16)
    h = _dot(xn, wup_ref[:, 0:D_FF])
    gate = _dot(xn, wup_ref[:, D_FF:2 * D_FF])
    rows = i * T + lax.broadcasted_iota(jnp.int32, (T, 1), 0)
    h = jnp.where(rows >= PAD, h, 0.0)
    hnew_ref[...] = h[T - FFN_HIST:, :]
    acts = []
    for c in range(N_FCH):
        sl = slice(c * LANES, (c + 1) * LANES)
        hc = h[:, sl]
        hbuf[c, FFN_HIST:FFN_HIST + T, :] = hc
        cc = (hbuf[c, FFN_HIST - 2:FFN_HIST - 2 + T, :] * fdw_ref[0:1, sl]
              + hbuf[c, FFN_HIST - 1:FFN_HIST - 1 + T, :] * fdw_ref[1:2, sl]
              + hc * fdw_ref[2:3, sl] + fdb_ref[:, sl])
        acts.append((_gelu_tanh(cc) * gate[:, sl]).astype(BF16))
        hbuf[c, 0:FFN_HIST, :] = hbuf[c, T:T + FFN_HIST, :]
    y = x + _dot(jnp.concatenate(acts, axis=1), wdn_ref[...])
    if final:
        y = _rmsnorm(y, gfin_ref[...])
    xo_ref[...] = y


def _pffn_call(final, n_tiles, xm, gffn, wup, fdw, fdb, wdn, gfin):
    T = TILE
    in_specs = [
        pl.BlockSpec((T, D_MODEL), lambda i: (i, 0)),
        _const_spec((1, D_MODEL)),
        _const_spec((D_MODEL, 2 * D_FF)),
        _const_spec((FFN_K, D_FF)),
        _const_spec((1, D_FF)),
        _const_spec((D_FF, D_MODEL)),
    ]
    args = [xm, gffn, wup, fdw, fdb, wdn]
    if final:
        in_specs.append(_const_spec((1, D_MODEL)))
        args.append(gfin)
        x_out = jax.ShapeDtypeStruct(((n_tiles - 1) * T, D_MODEL), F32)
        x_out_spec = pl.BlockSpec((T, D_MODEL), lambda i: (jnp.maximum(i - 1, 0), 0))
    else:
        x_out = jax.ShapeDtypeStruct((n_tiles * T, D_MODEL), F32)
        x_out_spec = pl.BlockSpec((T, D_MODEL), lambda i: (i, 0))
    return pl.pallas_call(
        functools.partial(_pffn_kernel, final),
        grid=(n_tiles,),
        in_specs=in_specs,
        out_specs=(x_out_spec, pl.BlockSpec((FFN_HIST, D_FF), lambda i: (0, 0))),
        out_shape=(x_out, jax.ShapeDtypeStruct((FFN_HIST, D_FF), F32)),
        scratch_shapes=[pltpu.VMEM((N_FCH, FFN_HIST + T, LANES), F32)],
        compiler_params=pltpu.CompilerParams(
            dimension_semantics=("arbitrary",), vmem_limit_bytes=VMEM_LIMIT),
        name="prompt_ffn",
    )(*args)


def _smix_kernel(xs_ref, gmix_ref, win_ref, dw_ref, db_ref, lng_ref, lnb_ref, wpw_ref, sinkc_ref,
                 wao_ref, wout_ref, sc_ref, ck_ref, cv_ref,
                 xo_ref, nk_ref, nv_ref, nc_ref,
                 u_s, q_s, k_s, v_s, gc_s, ga_s, c_s, a_s):
    i = pl.program_id(0)
    n_steps = pl.num_programs(0)

    @pl.when(i == 0)
    def _():
        xn = _rmsnorm(xs_ref[...], gmix_ref[...]).astype(BF16)

        def proj(lo, hi):
            return _dot(xn, win_ref[:, lo:hi])

        u_s[...] = proj(O_UA, O_UB) * _sigmoid(proj(O_UB, O_Q))
        q_s[...] = proj(O_Q, O_K)
        k_s[...] = proj(O_K, O_V)
        v_s[...] = proj(O_V, O_GC)
        gc_s[...] = _sigmoid(proj(O_GC, O_GA))
        ga_s[...] = _sigmoid(proj(O_GA, O_END))

    b0 = pl.multiple_of(i * SB, SB)
    u_blk = u_s[pl.ds(b0, SB), :]
    q_blk = q_s[pl.ds(b0, SB), :]
    k_blk = k_s[pl.ds(b0, SB), :]
    v_blk = v_s[pl.ds(b0, SB), :]
    lane_kh = lax.broadcasted_iota(jnp.int32, (1, D_KV), 1) // HEAD_DIM
    sink = sinkc_ref[:, 0:1]
    dw_hist = dw_ref[0:CONV_K - 1, :]
    dw_last = dw_ref[CONV_K - 1:CONV_K, :]
    crows, arows = [], []
    for bb in range(SB):
        st = sc_ref[bb]
        urow = u_blk[bb:bb + 1, :]
        crows.append(jnp.sum(st * dw_hist, axis=0, keepdims=True) + urow * dw_last + db_ref[...])
        nc_ref[bb, 0:CONV_K - 2, :] = st[1:CONV_K - 1, :]
        nc_ref[bb, CONV_K - 2:CONV_K - 1, :] = urow
        qrow = q_blk[bb:bb + 1, :]
        qm = jnp.concatenate(
            [jnp.where(lane_kh == kh, qrow[:, g * D_KV:(g + 1) * D_KV], 0.0)
             for g in range(GROUP) for kh in range(N_KV)], axis=0)
        kc = ck_ref[bb]
        vc = cv_ref[bb]
        knew = k_blk[bb:bb + 1, :]
        vnew = v_blk[bb:bb + 1, :]
        s = _dot_nt(qm.astype(BF16), kc.astype(BF16))
        snew = jnp.sum(qm * knew, axis=-1, keepdims=True)
        m = jnp.maximum(jnp.maximum(jnp.max(s, axis=-1, keepdims=True), snew), sink)
        e = jnp.exp(s - m)
        en = jnp.exp(snew - m)
        inv = 1.0 / (jnp.sum(e, axis=-1, keepdims=True) + en + jnp.exp(sink - m))
        o = _dot((e * inv).astype(BF16), vc.astype(BF16)) + (en * inv) * vnew
        slabs = []
        for g in range(GROUP):
            acc = None
            for kh in range(N_KV):
                r = g * N_KV + kh
                t = jnp.where(lane_kh == kh, o[r:r + 1, :], 0.0)
                acc = t if acc is None else acc + t
            slabs.append(acc)
        arows.append(jnp.concatenate(slabs, axis=1))
        nk_ref[bb, 0:WINDOW - 1, :] = kc[1:WINDOW, :]
        nk_ref[bb, WINDOW - 1:WINDOW, :] = knew
        nv_ref[bb, 0:WINDOW - 1, :] = vc[1:WINDOW, :]
        nv_ref[bb, WINDOW - 1:WINDOW, :] = vnew
    c_s[pl.ds(b0, SB), :] = jnp.concatenate(crows, axis=0)
    a_s[pl.ds(b0, SB), :] = jnp.concatenate(arows, axis=0)

    @pl.when(i == n_steps - 1)
    def _():
        y = _layernorm(c_s[...], lng_ref[...], lnb_ref[...])
        y = y * _sigmoid(y)
        cbr = _dot(y.astype(BF16), wpw_ref[...])
        abr = _dot(a_s[...].astype(BF16), wao_ref[...])
        mix = gc_s[...] * cbr + ga_s[...] * abr
        xo_ref[...] = xs_ref[...] + _dot(mix.astype(BF16), wout_ref[...])


def _smix_call(xs, gmix, win, dw, db, lng, lnb, wpw, sinkc, wao, wout, sconv, ck, cv):
    nb = xs.shape[0]
    in_specs = [
        _const_spec((nb, D_MODEL)),
        _const_spec((1, D_MODEL)),
        _const_spec((D_MODEL, O_END)),
        _const_spec((CONV_K, D_MODEL)),
        _const_spec((1, D_MODEL)),
        _const_spec((1, D_MODEL)),
        _const_spec((1, D_MODEL)),
        _const_spec((D_MODEL, D_MODEL)),
        _const_spec((GROUP * N_KV, LANES)),
        _const_spec((D_MODEL, D_MODEL)),
        _const_spec((D_MODEL, D_MODEL)),
        pl.BlockSpec((SB, CONV_K - 1, D_MODEL), lambda i: (i, 0, 0)),
        pl.BlockSpec((SB, WINDOW, D_KV), lambda i: (i, 0, 0)),
        pl.BlockSpec((SB, WINDOW, D_KV), lambda i: (i, 0, 0)),
    ]
    out_shape = (
        jax.ShapeDtypeStruct((nb, D_MODEL), F32),
        jax.ShapeDtypeStruct((nb, WINDOW, D_KV), F32),
        jax.ShapeDtypeStruct((nb, WINDOW, D_KV), F32),
        jax.ShapeDtypeStruct((nb, CONV_K - 1, D_MODEL), F32),
    )
    out_specs = (
        pl.BlockSpec((nb, D_MODEL), lambda i: (0, 0)),
        pl.BlockSpec((SB, WINDOW, D_KV), lambda i: (i, 0, 0)),
        pl.BlockSpec((SB, WINDOW, D_KV), lambda i: (i, 0, 0)),
        pl.BlockSpec((SB, CONV_K - 1, D_MODEL), lambda i: (i, 0, 0)),
    )
    scratch = [
        pltpu.VMEM((nb, D_MODEL), F32),
        pltpu.VMEM((nb, D_MODEL), F32),
        pltpu.VMEM((nb, D_KV), F32),
        pltpu.VMEM((nb, D_KV), F32),
        pltpu.VMEM((nb, D_MODEL), F32),
        pltpu.VMEM((nb, D_MODEL), F32),
        pltpu.VMEM((nb, D_MODEL), F32),
        pltpu.VMEM((nb, D_MODEL), F32),
    ]
    return pl.pallas_call(
        _smix_kernel,
        grid=(nb // SB,),
        in_specs=in_specs,
        out_specs=out_specs,
        out_shape=out_shape,
        scratch_shapes=scratch,
        compiler_params=pltpu.CompilerParams(
            dimension_semantics=("arbitrary",), vmem_limit_bytes=VMEM_LIMIT),
        name="sample_mixer",
    )(xs, gmix, win, dw, db, lng, lnb, wpw, sinkc, wao, wout, sconv, ck, cv)


def _sffn_kernel(final, *refs):
    if final:
        (x_ref, gffn_ref, wup_ref, s0_ref, s1_ref, fdw_ref, fdb_ref, wdn_ref, gfin_ref,
         xo_ref, h_ref) = refs
    else:
        (x_ref, gffn_ref, wup_ref, s0_ref, s1_ref, fdw_ref, fdb_ref, wdn_ref,
         xo_ref, h_ref) = refs
    x = x_ref[...]
    xn = _rmsnorm(x, gffn_ref[...]).astype(BF16)
    h = _dot(xn, wup_ref[:, 0:D_FF])
    gate = _dot(xn, wup_ref[:, D_FF:2 * D_FF])
    h_ref[...] = h
    cc = (s0_ref[...] * fdw_ref[0:1, :] + s1_ref[...] * fdw_ref[1:2, :]
          + h * fdw_ref[2:3, :] + fdb_ref[...])
    y = x + _dot((_gelu_tanh(cc) * gate).astype(BF16), wdn_ref[...])
    if final:
        y = _rmsnorm(y, gfin_ref[...])
    xo_ref[...] = y


def _sffn_call(final, x, gffn, wup, s0, s1, fdw, fdb, wdn, gfin):
    nb = x.shape[0]
    args = [x, gffn, wup, s0, s1, fdw, fdb, wdn]
    if final:
        args.append(gfin)
    return pl.pallas_call(
        functools.partial(_sffn_kernel, final),
        out_shape=(jax.ShapeDtypeStruct((nb, D_MODEL), F32),
                   jax.ShapeDtypeStruct((nb, D_FF), F32)),
        compiler_params=pltpu.CompilerParams(vmem_limit_bytes=VMEM_LIMIT),
        name="sample_ffn",
    )(*args)


def _group_major(w, axis):
    shape = w.shape
    w = w.reshape(shape[:axis] + (N_KV, GROUP, HEAD_DIM) + shape[axis + 1:])
    w = jnp.swapaxes(w, axis, axis + 1)
    return w.reshape(shape)


def kernel(x_prompt, x_sample, cache_swa_k, cache_swa_v, state_conv, state_ffn_conv, meta_tokens, norm_mix, w_in, conv_dw, conv_db, conv_ln_g, conv_ln_b, w_conv_pw, attn_sinks, w_attn_o, w_out, norm_ffn, w_ffn_up, ffn_dw, ffn_db, w_ffn_down, norm_final):
    depth = w_in.shape[0]
    batch, seq, _ = x_prompt.shape
    assert batch == 1 and seq % TILE == 0
    n_tiles = 1 + seq // TILE
    nb = x_sample.shape[0]

    wq = _group_major(w_in[:, :, O_Q:O_K], 2) * ATT_SCALE
    win_b = jnp.concatenate([w_in[:, :, :O_Q], wq, w_in[:, :, O_K:]], axis=2).astype(BF16)
    wpw_b = w_conv_pw.astype(BF16)
    wao_b = _group_major(w_attn_o, 1).astype(BF16)
    wout_b = w_out.astype(BF16)
    wup_b = w_ffn_up.astype(BF16)
    wdn_b = w_ffn_down.astype(BF16)
    sinks_gk = jnp.swapaxes(attn_sinks.reshape(depth, N_KV, GROUP), 1, 2).reshape(depth, GROUP * N_KV)
    sinks_col = jnp.broadcast_to(sinks_gk[:, :, None], (depth, GROUP * N_KV, LANES))
    dwb = conv_dw.reshape(depth, CONV_K, N_CCH, LANES).transpose(0, 2, 1, 3)
    dwb = jnp.repeat(dwb, SUBLANES, axis=2)
    row = lambda a: a[:, None, :]
    gmix, db, lng, lnb, gffn, fdb = map(row, (norm_mix, conv_db, conv_ln_g, conv_ln_b, norm_ffn, ffn_db))
    gfin = norm_final[None, :]

    x_first = jnp.concatenate([jnp.zeros((PAD, D_MODEL), F32), meta_tokens.astype(F32)], axis=0)
    xp = x_prompt[0]
    xs = x_sample[:, 0, :]
    ck_all = cache_swa_k.reshape(depth, nb, WINDOW, D_KV)
    cv_all = cache_swa_v.reshape(depth, nb, WINDOW, D_KV)

    kp, vp, cp, fp, ks, vs, cs, fs = [], [], [], [], [], [], [], []
    for l in range(depth):
        final = l == depth - 1
        xm, knew, vnew, unew = _pmix_call(
            l == 0, n_tiles, x_first, xp, gmix[l], win_b[l], dwb[l], db[l], lng[l], lnb[l],
            wpw_b[l], sinks_gk[l], wao_b[l], wout_b[l])
        xp, hnew = _pffn_call(final, n_tiles, xm, gffn[l], wup_b[l], ffn_dw[l], fdb[l], wdn_b[l], gfin)
        kp.append(knew)
        vp.append(vnew)
        cp.append(unew[CONV_HIST - (CONV_K - 1):])
        fp.append(hnew[FFN_HIST - (FFN_K - 1):])

        xsm, nk, nv, nc = _smix_call(
            xs, gmix[l], win_b[l], conv_dw[l], db[l], lng[l], lnb[l], wpw_b[l], sinks_col[l],
            wao_b[l], wout_b[l], state_conv[l], ck_all[l], cv_all[l])
        s0 = state_ffn_conv[l, :, 0, :]
        s1 = state_ffn_conv[l, :, 1, :]
        xs, hs = _sffn_call(final, xsm, gffn[l], wup_b[l], s0, s1, ffn_dw[l], fdb[l], wdn_b[l], gfin)
        ks.append(nk)
        vs.append(nv)
        cs.append(nc)
        fs.append(jnp.stack([s1, hs], axis=1))

    kv5 = lambda a, b: jnp.stack(a).reshape(depth, b, WINDOW, N_KV, HEAD_DIM)
    return (xp[None], xs[:, None, :],
            kv5(kp, 1), kv5(vp, 1), jnp.stack(cp)[:, None], jnp.stack(fp)[:, None],
            kv5(ks, nb), kv5(vs, nb), jnp.stack(cs), jnp.stack(fs))
```

```python
import functools
import math

import jax
import jax.numpy as jnp
from jax import lax
from jax.experimental import pallas as pl
from jax.experimental.pallas import tpu as pltpu

F32 = jnp.float32
BF16 = jnp.bfloat16

D_MODEL = 1024
N_META = 16
HEAD_DIM = 64
N_KV = 4
GROUP = 4
D_KV = N_KV * HEAD_DIM
WINDOW = 128
CONV_K = 31
D_FF = 2816
FFN_K = 3
EPS = 1e-6
NEG = -1e30
ATT_SCALE = 1.0 / math.sqrt(HEAD_DIM)
LOG2E = math.log2(math.e)

LANES = 128
SUBLANES = 8
N_CCH = D_MODEL // LANES
N_FCH = D_FF // LANES

TILE = 512
PAD = TILE - N_META
CONV_HIST = 32
CCHUNK = 256
CONV_RB = 64
FFN_HIST = 8
ABLK = 128
SB = 8

O_UA, O_UB, O_Q, O_K, O_V, O_GC, O_GA, O_END = 0, 1024, 2048, 3072, 3328, 3584, 4608, 5632

VMEM_LIMIT = 60 * 1024 * 1024


def _sigmoid(x):
    return 1.0 / (1.0 + jnp.exp(-x))


def _gelu_tanh(x):
    return 0.5 * x * (1.0 + jnp.tanh(0.7978845608028654 * (x + 0.044715 * (x * x * x))))


def _rmsnorm(x, g):
    ms = jnp.mean(x * x, axis=-1, keepdims=True)
    return x * lax.rsqrt(ms + EPS) * g


def _layernorm(x, g, b):
    mu = jnp.mean(x, axis=-1, keepdims=True)
    d = x - mu
    var = jnp.mean(d * d, axis=-1, keepdims=True)
    return d * lax.rsqrt(var + EPS) * g + b


def _dot(a, b):
    return jnp.dot(a, b, preferred_element_type=F32)


def _dot_nt(a, b):
    return lax.dot_general(a, b, (((1,), (1,)), ((), ())), preferred_element_type=F32)


def _conv_lane_chunk(ubuf, dwb_ref, db_ref, c_s, c):
    first_start = CONV_HIST - (CONV_K - 1)
    bias = jnp.broadcast_to(db_ref[:, c * LANES:(c + 1) * LANES], (SUBLANES, LANES))
    for base in range(0, TILE, CONV_RB):
        groups = range(base, base + CONV_RB, SUBLANES)
        acc = {}
        for phase in range(SUBLANES):
            ws = {j: dwb_ref[c, j * SUBLANES:(j + 1) * SUBLANES, :]
                  for j in range(phase, CONV_K, SUBLANES)}
            starts = sorted({first_start + r0 + j for r0 in groups for j in ws})
            for start in starts:
                win = ubuf[c, start:start + SUBLANES, :]
                for r0 in groups:
                    j = start - first_start - r0
                    if j in ws:
                        t = win * ws[j]
                        key = (r0, phase % 2)
                        acc[key] = t if key not in acc else acc[key] + t
        for r0 in groups:
            c_s[r0:r0 + SUBLANES, c * LANES:(c + 1) * LANES] = (
                (acc[(r0, 0)] + bias) + acc[(r0, 1)])


def _pmix_kernel(first, *refs):
    if first:
        xf_ref, refs = refs[0], refs[1:]
    (xm_ref, gmix_ref, win_ref, wq_ref, dwb_ref, db_ref, lng_ref, lnb_ref, wpw_ref, sinks_ref,
     wao_ref, wout_ref,
     xo_ref, knew_ref, vnew_ref, unew_ref,
     ubuf, uh_s, c_s, q_s, k_s, v_s, kh_s, vh_s, a_s) = refs
    T = TILE
    i = pl.program_id(0)
    slot = i % 2

    @pl.when(i == 0)
    def _():
        uh_s[0] = jnp.zeros((N_CCH, CONV_HIST, LANES), F32)
        kh_s[0] = jnp.zeros((N_KV, WINDOW, D_KV), BF16)
        vh_s[0] = jnp.zeros((N_KV, WINDOW, D_KV), BF16)

    ubuf[:, 0:CONV_HIST, :] = uh_s[slot]

    x = xm_ref[...]
    if first:
        x = jnp.where(i == 0, xf_ref[...], x)
    xn = _rmsnorm(x, gmix_ref[...]).astype(BF16)

    def proj(lo, hi):
        return _dot(xn, win_ref[:, lo:hi])

    rows = i * T + lax.broadcasted_iota(jnp.int32, (T, 1), 0)
    real = rows >= PAD
    lane_kh = lax.broadcasted_iota(jnp.int32, (1, D_KV), 1) // HEAD_DIM
    lanes_per_cw = CCHUNK // LANES

    def glu(cw):
        lo = cw * CCHUNK
        u = proj(O_UA + lo, O_UA + lo + CCHUNK) * _sigmoid(proj(O_UB + lo, O_UB + lo + CCHUNK))
        u = jnp.where(real, u, 0.0)
        unew_ref[:, lo:lo + CCHUNK] = u[T - CONV_HIST:, :]
        for cc in range(lanes_per_cw):
            uc = u[:, cc * LANES:(cc + 1) * LANES]
            ubuf[cw * lanes_per_cw + cc, CONV_HIST:CONV_HIST + T, :] = uc
            uh_s[1 - slot, cw * lanes_per_cw + cc] = uc[T - CONV_HIST:, :]

    def conv(cw):
        for cc in range(lanes_per_cw):
            _conv_lane_chunk(ubuf, dwb_ref, db_ref, c_s, cw * lanes_per_cw + cc)

    def proj_q(lo, hi):
        q_s[:, lo:hi] = _dot(xn, wq_ref[:, lo:hi]).astype(BF16)

    def proj_kv():
        kv = proj(O_K, O_GC)
        k = kv[:, 0:D_KV]
        v = kv[:, D_KV:2 * D_KV]
        knew_ref[...] = k[T - WINDOW:, :]
        vnew_ref[...] = v[T - WINDOW:, :]
        for kh in range(N_KV):
            sel = lane_kh == kh
            km = jnp.where(sel, k, 0.0).astype(BF16)
            vm = jnp.where(sel, v, 0.0).astype(BF16)
            k_s[kh] = km
            v_s[kh] = vm
            kh_s[1 - slot, kh] = km[T - WINDOW:, :]
            vh_s[1 - slot, kh] = vm[T - WINDOW:, :]

    glu(0)
    glu(1)
    proj_q(0, 2 * D_KV)
    conv(0)
    glu(2)
    proj_q(2 * D_KV, 4 * D_KV)
    conv(1)
    glu(3)
    proj_kv()
    conv(2)
    sgc = _sigmoid(proj(O_GC, O_GA))
    conv(3)
    sga = _sigmoid(proj(O_GA, O_END))

    qi = lax.broadcasted_iota(jnp.int32, (ABLK, 2 * ABLK), 0)
    kj = lax.broadcasted_iota(jnp.int32, (ABLK, 2 * ABLK), 1)
    band = (kj >= qi) & (kj <= qi + WINDOW)

    def window(cur, hist, kh, r0):
        if r0 == 0:
            return jnp.concatenate([hist[slot, kh], cur[kh, 0:ABLK, :]], axis=0)
        return cur[kh, r0 - ABLK:r0 + ABLK, :]

    def attn_block(r0):
        ok = band & (i * T + r0 - WINDOW + kj >= PAD)
        qb = q_s[r0:r0 + ABLK, :]
        qall = jnp.concatenate([qb[:, g * D_KV:(g + 1) * D_KV] for g in range(GROUP)], axis=0)
        probs = [[None] * N_KV for _ in range(GROUP)]
        for kh in range(N_KV):
            s = _dot_nt(qall, window(k_s, kh_s, kh, r0))
            for g in range(GROUP):
                sink = sinks_ref[g * N_KV + kh]
                sh = jnp.where(ok, s[g * ABLK:(g + 1) * ABLK, :], NEG)
                m = jnp.maximum(jnp.max(sh, axis=-1, keepdims=True), sink)
                e = jnp.exp2(sh - m)
                l = jnp.sum(e, axis=-1, keepdims=True) + jnp.exp2(sink - m)
                probs[g][kh] = (e * (1.0 / l)).astype(BF16)
        vst = jnp.concatenate([window(v_s, vh_s, kh, r0) for kh in range(N_KV)], axis=0)
        for g in range(GROUP):
            o = _dot(jnp.concatenate(probs[g], axis=1), vst)
            a_s[r0:r0 + ABLK, g * D_KV:(g + 1) * D_KV] = o.astype(BF16)

    attn_block(0)
    y = _layernorm(c_s[...], lng_ref[...], lnb_ref[...])
    y = (y * _sigmoid(y)).astype(BF16)
    n_blk = T // ABLK
    pw_cols = D_MODEL // n_blk
    cbr = []
    for b in range(n_blk):
        cbr.append(_dot(y, wpw_ref[:, b * pw_cols:(b + 1) * pw_cols]))
        if b + 1 < n_blk:
            attn_block((b + 1) * ABLK)
    cbr = jnp.concatenate(cbr, axis=1)
    abr = _dot(a_s[...], wao_ref[...])

    mix = sgc * cbr + sga * abr
    xo_ref[...] = x + _dot(mix.astype(BF16), wout_ref[...])


def _layer_spec(l, shape):
    zeros = (0,) * len(shape)
    return pl.BlockSpec((None,) + tuple(shape), lambda i: (l,) + zeros,
                        pipeline_mode=pl.Buffered(1))


def _const_spec(shape):
    zeros = (0,) * len(shape)
    return pl.BlockSpec(shape, lambda i: zeros, pipeline_mode=pl.Buffered(1))


def _pmix_call(l, n_tiles, x_first, x_main, gmix, win, wq, dwb, db, lng, lnb, wpw, sinks, wao, wout):
    T = TILE
    first = l == 0
    if first:
        x_spec = pl.BlockSpec((T, D_MODEL), lambda i: (jnp.maximum(i - 1, 0), 0))
    else:
        x_spec = pl.BlockSpec((T, D_MODEL), lambda i: (i, 0))
    in_specs = [
        x_spec,
        _layer_spec(l, (1, D_MODEL)),
        _layer_spec(l, (D_MODEL, O_END)),
        _layer_spec(l, (D_MODEL, D_MODEL)),
        _layer_spec(l, (N_CCH, CONV_K * SUBLANES, LANES)),
        _layer_spec(l, (1, D_MODEL)),
        _layer_spec(l, (1, D_MODEL)),
        _layer_spec(l, (1, D_MODEL)),
        _layer_spec(l, (D_MODEL, D_MODEL)),
        pl.BlockSpec(memory_space=pltpu.SMEM),
        _layer_spec(l, (D_MODEL, D_MODEL)),
        _layer_spec(l, (D_MODEL, D_MODEL)),
    ]
    args = [x_main, gmix, win, wq, dwb, db, lng, lnb, wpw, sinks, wao, wout]
    if first:
        in_specs = [_const_spec((T, D_MODEL))] + in_specs
        args = [x_first] + args
    out_shape = (
        jax.ShapeDtypeStruct((n_tiles * T, D_MODEL), F32),
        jax.ShapeDtypeStruct((WINDOW, D_KV), F32),
        jax.ShapeDtypeStruct((WINDOW, D_KV), F32),
        jax.ShapeDtypeStruct((CONV_HIST, D_MODEL), F32),
    )
    out_specs = (
        pl.BlockSpec((T, D_MODEL), lambda i: (i, 0)),
        pl.BlockSpec((WINDOW, D_KV), lambda i: (0, 0)),
        pl.BlockSpec((WINDOW, D_KV), lambda i: (0, 0)),
        pl.BlockSpec((CONV_HIST, D_MODEL), lambda i: (0, 0)),
    )
    scratch = [
        pltpu.VMEM((N_CCH, CONV_HIST + T, LANES), F32),
        pltpu.VMEM((2, N_CCH, CONV_HIST, LANES), F32),
        pltpu.VMEM((T, D_MODEL), F32),
        pltpu.VMEM((T, D_MODEL), BF16),
        pltpu.VMEM((N_KV, T, D_KV), BF16),
        pltpu.VMEM((N_KV, T, D_KV), BF16),
        pltpu.VMEM((2, N_KV, WINDOW, D_KV), BF16),
        pltpu.VMEM((2, N_KV, WINDOW, D_KV), BF16),
        pltpu.VMEM((T, D_MODEL), BF16),
    ]
    return pl.pallas_call(
        functools.partial(_pmix_kernel, first),
        grid=(n_tiles,),
        in_specs=in_specs,
        out_specs=out_specs,
        out_shape=out_shape,
        scratch_shapes=scratch,
        compiler_params=pltpu.CompilerParams(
            dimension_semantics=("arbitrary",), vmem_limit_bytes=VMEM_LIMIT),
        name="prompt_mixer",
    )(*args)


def _pffn_kernel(final, *refs):
    if final:
        (xm_ref, gffn_ref, wup_ref, fdw_ref, fdb_ref, wdn_ref, gfin_ref,
         xo_ref, hnew_ref, hbuf) = refs
    else:
        (xm_ref, gffn_ref, wup_ref, fdw_ref, fdb_ref, wdn_ref,
         xo_ref, hnew_ref, hbuf) = refs
    T = TILE
    i = pl.program_id(0)

    @pl.when(i == 0)
    def _():
        hbuf[:, 0:FFN_HIST, :] = jnp.zeros((N_FCH, FFN_HIST, LANES), F32)

    x = xm_ref[...]
    xn = _rmsnorm(x, gffn_ref[...]).astype(BF16)
    h = _dot(xn, wup_ref[:, 0:D_FF])
    gate = _dot(xn, wup_ref[:, D_FF:2 * D_FF])
    rows = i * T + lax.broadcasted_iota(jnp.int32, (T, 1), 0)
    h = jnp.where(rows >= PAD, h, 0.0)
    hnew_ref[...] = h[T - FFN_HIST:, :]
    acts = []
    for c in range(N_FCH):
        sl = slice(c * LANES, (c + 1) * LANES)
        hc = h[:, sl]
        hbuf[c, FFN_HIST:FFN_HIST + T, :] = hc
        cc = (hbuf[c, FFN_HIST - 2:FFN_HIST - 2 + T, :] * fdw_ref[0:1, sl]
              + hbuf[c, FFN_HIST - 1:FFN_HIST - 1 + T, :] * fdw_ref[1:2, sl]
              + hc * fdw_ref[2:3, sl] + fdb_ref[:, sl])
        acts.append((_gelu_tanh(cc) * gate[:, sl]).astype(BF16))
        hbuf[c, 0:FFN_HIST, :] = hbuf[c, T:T + FFN_HIST, :]
    y = x + _dot(jnp.concatenate(acts, axis=1), wdn_ref[...])
    if final:
        y = _rmsnorm(y, gfin_ref[...])
    xo_ref[...] = y


def _pffn_call(l, final, n_tiles, xm, gffn, wup, fdw, fdb, wdn, gfin):
    T = TILE
    in_specs = [
        pl.BlockSpec((T, D_MODEL), lambda i: (i, 0)),
        _layer_spec(l, (1, D_MODEL)),
        _layer_spec(l, (D_MODEL, 2 * D_FF)),
        _layer_spec(l, (FFN_K, D_FF)),
        _layer_spec(l, (1, D_FF)),
        _layer_spec(l, (D_FF, D_MODEL)),
    ]
    args = [xm, gffn, wup, fdw, fdb, wdn]
    if final:
        in_specs.append(_const_spec((1, D_MODEL)))
        args.append(gfin)
        x_out = jax.ShapeDtypeStruct(((n_tiles - 1) * T, D_MODEL), F32)
        x_out_spec = pl.BlockSpec((T, D_MODEL), lambda i: (jnp.maximum(i - 1, 0), 0))
    else:
        x_out = jax.ShapeDtypeStruct((n_tiles * T, D_MODEL), F32)
        x_out_spec = pl.BlockSpec((T, D_MODEL), lambda i: (i, 0))
    return pl.pallas_call(
        functools.partial(_pffn_kernel, final),
        grid=(n_tiles,),
        in_specs=in_specs,
        out_specs=(x_out_spec, pl.BlockSpec((FFN_HIST, D_FF), lambda i: (0, 0))),
        out_shape=(x_out, jax.ShapeDtypeStruct((FFN_HIST, D_FF), F32)),
        scratch_shapes=[pltpu.VMEM((N_FCH, FFN_HIST + T, LANES), F32)],
        compiler_params=pltpu.CompilerParams(
            dimension_semantics=("arbitrary",), vmem_limit_bytes=VMEM_LIMIT),
        name="prompt_ffn",
    )(*args)


def _smix_kernel(xs_ref, gmix_ref, win_ref, wq_ref, dw_ref, db_ref, lng_ref, lnb_ref, wpw_ref,
                 sinkc_ref, wao_ref, wout_ref, sc_ref, ck_ref, cv_ref,
                 xo_ref, nk_ref, nv_ref, nc_ref,
                 u_s, q_s, k_s, v_s, gc_s, ga_s, c_s, a_s):
    i = pl.program_id(0)
    n_steps = pl.num_programs(0)

    @pl.when(i == 0)
    def _():
        xn = _rmsnorm(xs_ref[...], gmix_ref[...]).astype(BF16)

        def proj(lo, hi):
            return _dot(xn, win_ref[:, lo:hi])

        u_s[...] = proj(O_UA, O_UB) * _sigmoid(proj(O_UB, O_Q))
        q_s[...] = _dot(xn, wq_ref[...])
        k_s[...] = proj(O_K, O_V)
        v_s[...] = proj(O_V, O_GC)
        gc_s[...] = _sigmoid(proj(O_GC, O_GA))
        ga_s[...] = _sigmoid(proj(O_GA, O_END))

    b0 = pl.multiple_of(i * SB, SB)
    u_blk = u_s[pl.ds(b0, SB), :]
    q_blk = q_s[pl.ds(b0, SB), :]
    k_blk = k_s[pl.ds(b0, SB), :]
    v_blk = v_s[pl.ds(b0, SB), :]
    lane_kh = lax.broadcasted_iota(jnp.int32, (1, D_KV), 1) // HEAD_DIM
    sink = sinkc_ref[:, 0:1]
    dw_hist = dw_ref[0:CONV_K - 1, :]
    dw_last = dw_ref[CONV_K - 1:CONV_K, :]
    crows, arows = [], []
    for bb in range(SB):
        st = sc_ref[bb]
        urow = u_blk[bb:bb + 1, :]
        crows.append(jnp.sum(st * dw_hist, axis=0, keepdims=True) + urow * dw_last + db_ref[...])
        nc_ref[bb, 0:CONV_K - 2, :] = st[1:CONV_K - 1, :]
        nc_ref[bb, CONV_K - 2:CONV_K - 1, :] = urow
        qrow = q_blk[bb:bb + 1, :]
        qm = jnp.concatenate(
            [jnp.where(lane_kh == kh, qrow[:, g * D_KV:(g + 1) * D_KV], 0.0)
             for g in range(GROUP) for kh in range(N_KV)], axis=0)
        kc = ck_ref[bb]
        vc = cv_ref[bb]
        knew = k_blk[bb:bb + 1, :]
        vnew = v_blk[bb:bb + 1, :]
        s = _dot_nt(qm.astype(BF16), kc.astype(BF16))
        snew = jnp.sum(qm * knew, axis=-1, keepdims=True)
        m = jnp.maximum(jnp.maximum(jnp.max(s, axis=-1, keepdims=True), snew), sink)
        e = jnp.exp2(s - m)
        en = jnp.exp2(snew - m)
        inv = 1.0 / (jnp.sum(e, axis=-1, keepdims=True) + en + jnp.exp2(sink - m))
        o = _dot((e * inv).astype(BF16), vc.astype(BF16)) + (en * inv) * vnew
        slabs = []
        for g in range(GROUP):
            acc = None
            for kh in range(N_KV):
                r = g * N_KV + kh
                t = jnp.where(lane_kh == kh, o[r:r + 1, :], 0.0)
                acc = t if acc is None else acc + t
            slabs.append(acc)
        arows.append(jnp.concatenate(slabs, axis=1))
        nk_ref[bb, 0:WINDOW - 1, :] = kc[1:WINDOW, :]
        nk_ref[bb, WINDOW - 1:WINDOW, :] = knew
        nv_ref[bb, 0:WINDOW - 1, :] = vc[1:WINDOW, :]
        nv_ref[bb, WINDOW - 1:WINDOW, :] = vnew
    c_s[pl.ds(b0, SB), :] = jnp.concatenate(crows, axis=0)
    a_s[pl.ds(b0, SB), :] = jnp.concatenate(arows, axis=0)

    @pl.when(i == n_steps - 1)
    def _():
        y = _layernorm(c_s[...], lng_ref[...], lnb_ref[...])
        y = y * _sigmoid(y)
        cbr = _dot(y.astype(BF16), wpw_ref[...])
        abr = _dot(a_s[...].astype(BF16), wao_ref[...])
        mix = gc_s[...] * cbr + ga_s[...] * abr
        xo_ref[...] = xs_ref[...] + _dot(mix.astype(BF16), wout_ref[...])


def _smix_call(l, xs, gmix, win, wq, dw, db, lng, lnb, wpw, sinkc, wao, wout, sconv, ck, cv):
    nb = xs.shape[0]
    state_spec = lambda *dims: pl.BlockSpec((None, SB) + dims, lambda i: (l, i) + (0,) * len(dims))
    in_specs = [
        _const_spec((nb, D_MODEL)),
        _layer_spec(l, (1, D_MODEL)),
        _layer_spec(l, (D_MODEL, O_END)),
        _layer_spec(l, (D_MODEL, D_MODEL)),
        _layer_spec(l, (CONV_K, D_MODEL)),
        _layer_spec(l, (1, D_MODEL)),
        _layer_spec(l, (1, D_MODEL)),
        _layer_spec(l, (1, D_MODEL)),
        _layer_spec(l, (D_MODEL, D_MODEL)),
        _layer_spec(l, (GROUP * N_KV, LANES)),
        _layer_spec(l, (D_MODEL, D_MODEL)),
        _layer_spec(l, (D_MODEL, D_MODEL)),
        state_spec(CONV_K - 1, D_MODEL),
        state_spec(WINDOW, D_KV),
        state_spec(WINDOW, D_KV),
    ]
    out_shape = (
        jax.ShapeDtypeStruct((nb, D_MODEL), F32),
        jax.ShapeDtypeStruct((nb, WINDOW, D_KV), F32),
        jax.ShapeDtypeStruct((nb, WINDOW, D_KV), F32),
        jax.ShapeDtypeStruct((nb, CONV_K - 1, D_MODEL), F32),
    )
    out_specs = (
        pl.BlockSpec((nb, D_MODEL), lambda i: (0, 0)),
        pl.BlockSpec((SB, WINDOW, D_KV), lambda i: (i, 0, 0)),
        pl.BlockSpec((SB, WINDOW, D_KV), lambda i: (i, 0, 0)),
        pl.BlockSpec((SB, CONV_K - 1, D_MODEL), lambda i: (i, 0, 0)),
    )
    scratch = [
        pltpu.VMEM((nb, D_MODEL), F32),
        pltpu.VMEM((nb, D_MODEL), F32),
        pltpu.VMEM((nb, D_KV), F32),
        pltpu.VMEM((nb, D_KV), F32),
        pltpu.VMEM((nb, D_MODEL), F32),
        pltpu.VMEM((nb, D_MODEL), F32),
        pltpu.VMEM((nb, D_MODEL), F32),
        pltpu.VMEM((nb, D_MODEL), F32),
    ]
    return pl.pallas_call(
        _smix_kernel,
        grid=(nb // SB,),
        in_specs=in_specs,
        out_specs=out_specs,
        out_shape=out_shape,
        scratch_shapes=scratch,
        compiler_params=pltpu.CompilerParams(
            dimension_semantics=("arbitrary",), vmem_limit_bytes=VMEM_LIMIT),
        name="sample_mixer",
    )(xs, gmix, win, wq, dw, db, lng, lnb, wpw, sinkc, wao, wout, sconv, ck, cv)


def _sffn_kernel(final, *refs):
    if final:
        (x_ref, gffn_ref, wup_ref, s0_ref, s1_ref, fdw_ref, fdb_ref, wdn_ref, gfin_ref,
         xo_ref, h_ref) = refs
    else:
        (x_ref, gffn_ref, wup_ref, s0_ref, s1_ref, fdw_ref, fdb_ref, wdn_ref,
         xo_ref, h_ref) = refs
    x = x_ref[...]
    xn = _rmsnorm(x, gffn_ref[...]).astype(BF16)
    h = _dot(xn, wup_ref[:, 0:D_FF])
    gate = _dot(xn, wup_ref[:, D_FF:2 * D_FF])
    h_ref[...] = h
    cc = (s0_ref[...] * fdw_ref[0:1, :] + s1_ref[...] * fdw_ref[1:2, :]
          + h * fdw_ref[2:3, :] + fdb_ref[...])
    y = x + _dot((_gelu_tanh(cc) * gate).astype(BF16), wdn_ref[...])
    if final:
        y = _rmsnorm(y, gfin_ref[...])
    xo_ref[...] = y


def _sffn_call(l, final, x, gffn, wup, s0, s1, fdw, fdb, wdn, gfin):
    nb = x.shape[0]
    in_specs = [
        _const_spec((nb, D_MODEL)),
        _layer_spec(l, (1, D_MODEL)),
        _layer_spec(l, (D_MODEL, 2 * D_FF)),
        _const_spec((nb, D_FF)),
        _const_spec((nb, D_FF)),
        _layer_spec(l, (FFN_K, D_FF)),
        _layer_spec(l, (1, D_FF)),
        _layer_spec(l, (D_FF, D_MODEL)),
    ]
    args = [x, gffn, wup, s0, s1, fdw, fdb, wdn]
    if final:
        in_specs.append(_const_spec((1, D_MODEL)))
        args.append(gfin)
    return pl.pallas_call(
        functools.partial(_sffn_kernel, final),
        grid=(1,),
        in_specs=in_specs,
        out_specs=(pl.BlockSpec((nb, D_MODEL), lambda i: (0, 0)),
                   pl.BlockSpec((nb, D_FF), lambda i: (0, 0))),
        out_shape=(jax.ShapeDtypeStruct((nb, D_MODEL), F32),
                   jax.ShapeDtypeStruct((nb, D_FF), F32)),
        compiler_params=pltpu.CompilerParams(
            dimension_semantics=("arbitrary",), vmem_limit_bytes=VMEM_LIMIT),
        name="sample_ffn",
    )(*args)


def _group_major(w, axis):
    shape = w.shape
    w = w.reshape(shape[:axis] + (N_KV, GROUP, HEAD_DIM) + shape[axis + 1:])
    w = jnp.swapaxes(w, axis, axis + 1)
    return w.reshape(shape)


def kernel(x_prompt, x_sample, cache_swa_k, cache_swa_v, state_conv, state_ffn_conv, meta_tokens, norm_mix, w_in, conv_dw, conv_db, conv_ln_g, conv_ln_b, w_conv_pw, attn_sinks, w_attn_o, w_out, norm_ffn, w_ffn_up, ffn_dw, ffn_db, w_ffn_down, norm_final):
    depth = w_in.shape[0]
    batch, seq, _ = x_prompt.shape
    assert batch == 1 and seq % TILE == 0
    n_tiles = 1 + seq // TILE
    nb = x_sample.shape[0]

    win_b = w_in.astype(BF16)
    wq_b = (_group_major(w_in[:, :, O_Q:O_K], 2) * (ATT_SCALE * LOG2E)).astype(BF16)
    wpw_b = w_conv_pw.astype(BF16)
    wao_b = _group_major(w_attn_o, 1).astype(BF16)
    wout_b = w_out.astype(BF16)
    wup_b = w_ffn_up.astype(BF16)
    wdn_b = w_ffn_down.astype(BF16)
    sinks_gk = jnp.swapaxes(attn_sinks.reshape(depth, N_KV, GROUP), 1, 2).reshape(depth, GROUP * N_KV)
    sinks_gk = sinks_gk * LOG2E
    sinks_col = jnp.broadcast_to(sinks_gk[:, :, None], (depth, GROUP * N_KV, LANES))
    dwb = conv_dw.reshape(depth, CONV_K, N_CCH, LANES).transpose(0, 2, 1, 3)
    dwb = jnp.repeat(dwb, SUBLANES, axis=2)
    row = lambda a: a[:, None, :]
    gmix, db, lng, lnb, gffn, fdb = map(row, (norm_mix, conv_db, conv_ln_g, conv_ln_b, norm_ffn, ffn_db))
    gfin = norm_final[None, :]

    x_first = jnp.concatenate([jnp.zeros((PAD, D_MODEL), F32), meta_tokens.astype(F32)], axis=0)
    xp = x_prompt[0]
    xs = x_sample[:, 0, :]
    ck_all = cache_swa_k.reshape(depth, nb, WINDOW, D_KV)
    cv_all = cache_swa_v.reshape(depth, nb, WINDOW, D_KV)

    kp, vp, cp, fp, ks, vs, cs, fs = [], [], [], [], [], [], [], []
    for l in range(depth):
        final = l == depth - 1
        xm, knew, vnew, unew = _pmix_call(
            l, n_tiles, x_first, xp, gmix, win_b, wq_b, dwb, db, lng, lnb, wpw_b, sinks_gk[l],
            wao_b, wout_b)
        xp, hnew = _pffn_call(l, final, n_tiles, xm, gffn, wup_b, ffn_dw, fdb, wdn_b, gfin)
        kp.append(knew)
        vp.append(vnew)
        cp.append(unew[CONV_HIST - (CONV_K - 1):])
        fp.append(hnew[FFN_HIST - (FFN_K - 1):])

        xsm, nk, nv, nc = _smix_call(
            l, xs, gmix, win_b, wq_b, conv_dw, db, lng, lnb, wpw_b, sinks_col, wao_b, wout_b,
            state_conv, ck_all, cv_all)
        s0 = state_ffn_conv[l, :, 0, :]
        s1 = state_ffn_conv[l, :, 1, :]
        xs, hs = _sffn_call(l, final, xsm, gffn, wup_b, s0, s1, ffn_dw, fdb, wdn_b, gfin)
        ks.append(nk)
        vs.append(nv)
        cs.append(nc)
        fs.append(jnp.stack([s1, hs], axis=1))

    kv5 = lambda a, b: jnp.stack(a).reshape(depth, b, WINDOW, N_KV, HEAD_DIM)
    return (xp[None], xs[:, None, :],
            kv5(kp, 1), kv5(vp, 1), jnp.stack(cp)[:, None], jnp.stack(fp)[:, None],
            kv5(ks, nb), kv5(vs, nb), jnp.stack(cs), jnp.stack(fs))
```

```python
import functools
import math

import jax
import jax.numpy as jnp
from jax import lax
from jax.experimental import pallas as pl
from jax.experimental.pallas import tpu as pltpu

F32 = jnp.float32
BF16 = jnp.bfloat16

D_MODEL = 1024
N_META = 16
HEAD_DIM = 64
N_KV = 4
GROUP = 4
D_KV = N_KV * HEAD_DIM
WINDOW = 128
CONV_K = 31
D_FF = 2816
FFN_K = 3
EPS = 1e-6
NEG = -1e30
ATT_SCALE = 1.0 / math.sqrt(HEAD_DIM)
LOG2E = math.log2(math.e)

LANES = 128
SUBLANES = 8
N_CCH = D_MODEL // LANES
N_FCH = D_FF // LANES

TILE = 512
PAD = TILE - N_META
CONV_HIST = 32
CCHUNK = 256
CONV_RB = 64
FFN_HIST = 8
ABLK = 128
SB = 8

O_UA, O_UB, O_Q, O_K, O_V, O_GC, O_GA, O_END = 0, 1024, 2048, 3072, 3328, 3584, 4608, 5632

VMEM_LIMIT = 60 * 1024 * 1024


def _sigmoid(x):
    return 1.0 / (1.0 + jnp.exp(-x))


def _gelu_tanh(x):
    return 0.5 * x * (1.0 + jnp.tanh(0.7978845608028654 * (x + 0.044715 * (x * x * x))))


def _rmsnorm(x, g):
    ms = jnp.mean(x * x, axis=-1, keepdims=True)
    return x * lax.rsqrt(ms + EPS) * g


def _layernorm(x, g, b):
    mu = jnp.mean(x, axis=-1, keepdims=True)
    d = x - mu
    var = jnp.mean(d * d, axis=-1, keepdims=True)
    return d * lax.rsqrt(var + EPS) * g + b


def _dot(a, b):
    return jnp.dot(a, b, preferred_element_type=F32)


def _dot_nt(a, b):
    return lax.dot_general(a, b, (((1,), (1,)), ((), ())), preferred_element_type=F32)


def _conv_lane_chunk(ubuf, dwb_ref, db_ref, c_s, c):
    first_start = CONV_HIST - (CONV_K - 1)
    bias = jnp.broadcast_to(db_ref[:, c * LANES:(c + 1) * LANES], (SUBLANES, LANES))
    for base in range(0, TILE, CONV_RB):
        groups = range(base, base + CONV_RB, SUBLANES)
        acc = {}
        for phase in range(SUBLANES):
            ws = {j: dwb_ref[c, j * SUBLANES:(j + 1) * SUBLANES, :]
                  for j in range(phase, CONV_K, SUBLANES)}
            starts = sorted({first_start + r0 + j for r0 in groups for j in ws})
            for start in starts:
                win = ubuf[c, start:start + SUBLANES, :]
                for r0 in groups:
                    j = start - first_start - r0
                    if j in ws:
                        t = win * ws[j]
                        key = (r0, phase % 2)
                        acc[key] = t if key not in acc else acc[key] + t
        for r0 in groups:
            c_s[r0:r0 + SUBLANES, c * LANES:(c + 1) * LANES] = (
                (acc[(r0, 0)] + bias) + acc[(r0, 1)])


def _pmix_kernel(first, *refs):
    if first:
        xf_ref, refs = refs[0], refs[1:]
    (xm_ref, gmix_ref, win_ref, wq_ref, dwb_ref, db_ref, lng_ref, lnb_ref, wpw_ref, sinks_ref,
     wao_ref, wout_ref,
     xo_ref, knew_ref, vnew_ref, unew_ref,
     ubuf, uh_s, c_s, q_s, k_s, v_s, kh_s, vh_s, a_s) = refs
    T = TILE
    i = pl.program_id(0)
    slot = i % 2

    @pl.when(i == 0)
    def _():
        uh_s[0] = jnp.zeros((N_CCH, CONV_HIST, LANES), F32)
        kh_s[0] = jnp.zeros((N_KV, WINDOW, D_KV), BF16)
        vh_s[0] = jnp.zeros((N_KV, WINDOW, D_KV), BF16)

    ubuf[:, 0:CONV_HIST, :] = uh_s[slot]

    x = xm_ref[...]
    if first:
        x = jnp.where(i == 0, xf_ref[...], x)
    xn = _rmsnorm(x, gmix_ref[...]).astype(BF16)

    def proj(lo, hi):
        return _dot(xn, win_ref[:, lo:hi])

    rows = i * T + lax.broadcasted_iota(jnp.int32, (T, 1), 0)
    real = rows >= PAD
    lane_kh = lax.broadcasted_iota(jnp.int32, (1, D_KV), 1) // HEAD_DIM
    lanes_per_cw = CCHUNK // LANES

    def glu(cw):
        lo = cw * CCHUNK
        u = proj(O_UA + lo, O_UA + lo + CCHUNK) * _sigmoid(proj(O_UB + lo, O_UB + lo + CCHUNK))
        u = jnp.where(real, u, 0.0)
        unew_ref[:, lo:lo + CCHUNK] = u[T - CONV_HIST:, :]
        for cc in range(lanes_per_cw):
            uc = u[:, cc * LANES:(cc + 1) * LANES]
            ubuf[cw * lanes_per_cw + cc, CONV_HIST:CONV_HIST + T, :] = uc
            uh_s[1 - slot, cw * lanes_per_cw + cc] = uc[T - CONV_HIST:, :]

    def conv(cw):
        for cc in range(lanes_per_cw):
            _conv_lane_chunk(ubuf, dwb_ref, db_ref, c_s, cw * lanes_per_cw + cc)

    def proj_q(lo, hi):
        q_s[:, lo:hi] = _dot(xn, wq_ref[:, lo:hi]).astype(BF16)

    def proj_kv():
        kv = proj(O_K, O_GC)
        k = kv[:, 0:D_KV]
        v = kv[:, D_KV:2 * D_KV]
        knew_ref[...] = k[T - WINDOW:, :]
        vnew_ref[...] = v[T - WINDOW:, :]
        for kh in range(N_KV):
            sel = lane_kh == kh
            km = jnp.where(sel, k, 0.0).astype(BF16)
            vm = jnp.where(sel, v, 0.0).astype(BF16)
            k_s[kh] = km
            v_s[kh] = vm
            kh_s[1 - slot, kh] = km[T - WINDOW:, :]
            vh_s[1 - slot, kh] = vm[T - WINDOW:, :]

    glu(0)
    glu(1)
    proj_q(0, 2 * D_KV)
    conv(0)
    glu(2)
    proj_q(2 * D_KV, 4 * D_KV)
    conv(1)
    glu(3)
    proj_kv()
    conv(2)
    sgc = _sigmoid(proj(O_GC, O_GA))
    conv(3)
    sga = _sigmoid(proj(O_GA, O_END))

    qi = lax.broadcasted_iota(jnp.int32, (ABLK, 2 * ABLK), 0)
    kj = lax.broadcasted_iota(jnp.int32, (ABLK, 2 * ABLK), 1)
    band = (kj >= qi) & (kj <= qi + WINDOW)

    def window(cur, hist, kh, r0):
        if r0 == 0:
            return jnp.concatenate([hist[slot, kh], cur[kh, 0:ABLK, :]], axis=0)
        return cur[kh, r0 - ABLK:r0 + ABLK, :]

    def attn_block(r0):
        ok = band & (i * T + r0 - WINDOW + kj >= PAD)
        qb = q_s[r0:r0 + ABLK, :]
        qall = jnp.concatenate([qb[:, g * D_KV:(g + 1) * D_KV] for g in range(GROUP)], axis=0)
        probs = [[None] * N_KV for _ in range(GROUP)]
        for kh in range(N_KV):
            s = _dot_nt(qall, window(k_s, kh_s, kh, r0))
            for g in range(GROUP):
                sink = sinks_ref[g * N_KV + kh]
                sh = jnp.where(ok, s[g * ABLK:(g + 1) * ABLK, :], NEG)
                m = jnp.maximum(jnp.max(sh, axis=-1, keepdims=True), sink)
                e = jnp.exp2(sh - m)
                l = jnp.sum(e, axis=-1, keepdims=True) + jnp.exp2(sink - m)
                probs[g][kh] = (e * (1.0 / l)).astype(BF16)
        vst = jnp.concatenate([window(v_s, vh_s, kh, r0) for kh in range(N_KV)], axis=0)
        for g in range(GROUP):
            o = _dot(jnp.concatenate(probs[g], axis=1), vst)
            a_s[r0:r0 + ABLK, g * D_KV:(g + 1) * D_KV] = o.astype(BF16)

    attn_block(0)
    y = _layernorm(c_s[...], lng_ref[...], lnb_ref[...])
    y = (y * _sigmoid(y)).astype(BF16)
    n_blk = T // ABLK
    pw_cols = D_MODEL // n_blk
    cbr = []
    for b in range(n_blk):
        cbr.append(_dot(y, wpw_ref[:, b * pw_cols:(b + 1) * pw_cols]))
        if b + 1 < n_blk:
            attn_block((b + 1) * ABLK)
    cbr = jnp.concatenate(cbr, axis=1)
    abr = _dot(a_s[...], wao_ref[...])

    mix = sgc * cbr + sga * abr
    xo_ref[...] = x + _dot(mix.astype(BF16), wout_ref[...])


def _layer_spec(l, shape):
    zeros = (0,) * len(shape)
    return pl.BlockSpec((None,) + tuple(shape), lambda i: (l,) + zeros,
                        pipeline_mode=pl.Buffered(1))


def _const_spec(shape):
    zeros = (0,) * len(shape)
    return pl.BlockSpec(shape, lambda i: zeros, pipeline_mode=pl.Buffered(1))


def _pmix_call(l, n_tiles, x_first, x_main, gmix, win, wq, dwb, db, lng, lnb, wpw, sinks, wao, wout):
    T = TILE
    first = l == 0
    if first:
        x_spec = pl.BlockSpec((T, D_MODEL), lambda i: (jnp.maximum(i - 1, 0), 0))
    else:
        x_spec = pl.BlockSpec((T, D_MODEL), lambda i: (i, 0))
    in_specs = [
        x_spec,
        _layer_spec(l, (1, D_MODEL)),
        _layer_spec(l, (D_MODEL, O_END)),
        _layer_spec(l, (D_MODEL, D_MODEL)),
        _layer_spec(l, (N_CCH, CONV_K * SUBLANES, LANES)),
        _layer_spec(l, (1, D_MODEL)),
        _layer_spec(l, (1, D_MODEL)),
        _layer_spec(l, (1, D_MODEL)),
        _layer_spec(l, (D_MODEL, D_MODEL)),
        pl.BlockSpec(memory_space=pltpu.SMEM),
        _layer_spec(l, (D_MODEL, D_MODEL)),
        _layer_spec(l, (D_MODEL, D_MODEL)),
    ]
    args = [x_main, gmix, win, wq, dwb, db, lng, lnb, wpw, sinks, wao, wout]
    if first:
        in_specs = [_const_spec((T, D_MODEL))] + in_specs
        args = [x_first] + args
    out_shape = (
        jax.ShapeDtypeStruct((n_tiles * T, D_MODEL), F32),
        jax.ShapeDtypeStruct((WINDOW, D_KV), F32),
        jax.ShapeDtypeStruct((WINDOW, D_KV), F32),
        jax.ShapeDtypeStruct((CONV_HIST, D_MODEL), F32),
    )
    out_specs = (
        pl.BlockSpec((T, D_MODEL), lambda i: (i, 0)),
        pl.BlockSpec((WINDOW, D_KV), lambda i: (0, 0)),
        pl.BlockSpec((WINDOW, D_KV), lambda i: (0, 0)),
        pl.BlockSpec((CONV_HIST, D_MODEL), lambda i: (0, 0)),
    )
    scratch = [
        pltpu.VMEM((N_CCH, CONV_HIST + T, LANES), F32),
        pltpu.VMEM((2, N_CCH, CONV_HIST, LANES), F32),
        pltpu.VMEM((T, D_MODEL), F32),
        pltpu.VMEM((T, D_MODEL), BF16),
        pltpu.VMEM((N_KV, T, D_KV), BF16),
        pltpu.VMEM((N_KV, T, D_KV), BF16),
        pltpu.VMEM((2, N_KV, WINDOW, D_KV), BF16),
        pltpu.VMEM((2, N_KV, WINDOW, D_KV), BF16),
        pltpu.VMEM((T, D_MODEL), BF16),
    ]
    return pl.pallas_call(
        functools.partial(_pmix_kernel, first),
        grid=(n_tiles,),
        in_specs=in_specs,
        out_specs=out_specs,
        out_shape=out_shape,
        scratch_shapes=scratch,
        compiler_params=pltpu.CompilerParams(
            dimension_semantics=("arbitrary",), vmem_limit_bytes=VMEM_LIMIT),
        name="prompt_mixer",
    )(*args)


def _pffn_kernel(final, *refs):
    if final:
        (xm_ref, gffn_ref, wup_ref, fdw_ref, fdb_ref, wdn_ref, gfin_ref,
         xo_ref, hnew_ref, hbuf) = refs
    else:
        (xm_ref, gffn_ref, wup_ref, fdw_ref, fdb_ref, wdn_ref,
         xo_ref, hnew_ref, hbuf) = refs
    T = TILE
    i = pl.program_id(0)

    @pl.when(i == 0)
    def _():
        hbuf[:, 0:FFN_HIST, :] = jnp.zeros((N_FCH, FFN_HIST, LANES), F32)

    x = xm_ref[...]
    xn = _rmsnorm(x, gffn_ref[...]).astype(BF16)
    h = _dot(xn, wup_ref[:, 0:D_FF])
    gate = _dot(xn, wup_ref[:, D_FF:2 * D_FF])
    rows = i * T + lax.broadcasted_iota(jnp.int32, (T, 1), 0)
    h = jnp.where(rows >= PAD, h, 0.0)
    hnew_ref[...] = h[T - FFN_HIST:, :]
    acts = []
    for c in range(N_FCH):
        sl = slice(c * LANES, (c + 1) * LANES)
        hc = h[:, sl]
        hbuf[c, FFN_HIST:FFN_HIST + T, :] = hc
        cc = (hbuf[c, FFN_HIST - 2:FFN_HIST - 2 + T, :] * fdw_ref[0:1, sl]
              + hbuf[c, FFN_HIST - 1:FFN_HIST - 1 + T, :] * fdw_ref[1:2, sl]
              + hc * fdw_ref[2:3, sl] + fdb_ref[:, sl])
        acts.append((_gelu_tanh(cc) * gate[:, sl]).astype(BF16))
        hbuf[c, 0:FFN_HIST, :] = hbuf[c, T:T + FFN_HIST, :]
    y = x + _dot(jnp.concatenate(acts, axis=1), wdn_ref[...])
    if final:
        y = _rmsnorm(y, gfin_ref[...])
    xo_ref[...] = y


def _pffn_call(l, final, n_tiles, xm, gffn, wup, fdw, fdb, wdn, gfin):
    T = TILE
    in_specs = [
        pl.BlockSpec((T, D_MODEL), lambda i: (i, 0)),
        _layer_spec(l, (1, D_MODEL)),
        _layer_spec(l, (D_MODEL, 2 * D_FF)),
        _layer_spec(l, (FFN_K, D_FF)),
        _layer_spec(l, (1, D_FF)),
        _layer_spec(l, (D_FF, D_MODEL)),
    ]
    args = [xm, gffn, wup, fdw, fdb, wdn]
    if final:
        in_specs.append(_const_spec((1, D_MODEL)))
        args.append(gfin)
        x_out = jax.ShapeDtypeStruct(((n_tiles - 1) * T, D_MODEL), F32)
        x_out_spec = pl.BlockSpec((T, D_MODEL), lambda i: (jnp.maximum(i - 1, 0), 0))
    else:
        x_out = jax.ShapeDtypeStruct((n_tiles * T, D_MODEL), F32)
        x_out_spec = pl.BlockSpec((T, D_MODEL), lambda i: (i, 0))
    return pl.pallas_call(
        functools.partial(_pffn_kernel, final),
        grid=(n_tiles,),
        in_specs=in_specs,
        out_specs=(x_out_spec, pl.BlockSpec((FFN_HIST, D_FF), lambda i: (0, 0))),
        out_shape=(x_out, jax.ShapeDtypeStruct((FFN_HIST, D_FF), F32)),
        scratch_shapes=[pltpu.VMEM((N_FCH, FFN_HIST + T, LANES), F32)],
        compiler_params=pltpu.CompilerParams(
            dimension_semantics=("arbitrary",), vmem_limit_bytes=VMEM_LIMIT),
        name="prompt_ffn",
    )(*args)


def _smix_kernel(xs_ref, gmix_ref, win_ref, wq_ref, dw_ref, db_ref, lng_ref, lnb_ref, wpw_ref,
                 sinkc_ref, wao_ref, wout_ref, sc_ref, ck_ref, cv_ref,
                 xo_ref, nk_ref, nv_ref, nc_ref,
                 u_s, q_s, k_s, v_s, gc_s, ga_s, c_s, a_s):
    i = pl.program_id(0)
    n_steps = pl.num_programs(0)

    @pl.when(i == 0)
    def _():
        xn = _rmsnorm(xs_ref[...], gmix_ref[...]).astype(BF16)

        def proj(lo, hi):
            return _dot(xn, win_ref[:, lo:hi])

        u_s[...] = proj(O_UA, O_UB) * _sigmoid(proj(O_UB, O_Q))
        q_s[...] = _dot(xn, wq_ref[...])
        k_s[...] = proj(O_K, O_V)
        v_s[...] = proj(O_V, O_GC)
        gc_s[...] = _sigmoid(proj(O_GC, O_GA))
        ga_s[...] = _sigmoid(proj(O_GA, O_END))

    b0 = pl.multiple_of(i * SB, SB)
    u_blk = u_s[pl.ds(b0, SB), :]
    q_blk = q_s[pl.ds(b0, SB), :]
    k_blk = k_s[pl.ds(b0, SB), :]
    v_blk = v_s[pl.ds(b0, SB), :]
    lane_kh = lax.broadcasted_iota(jnp.int32, (1, D_KV), 1) // HEAD_DIM
    sink = sinkc_ref[:, 0:1]
    dw_hist = dw_ref[0:CONV_K - 1, :]
    dw_last = dw_ref[CONV_K - 1:CONV_K, :]
    crows, arows = [], []
    for bb in range(SB):
        st = sc_ref[bb]
        urow = u_blk[bb:bb + 1, :]
        crows.append(jnp.sum(st * dw_hist, axis=0, keepdims=True) + urow * dw_last + db_ref[...])
        nc_ref[bb, 0:CONV_K - 2, :] = st[1:CONV_K - 1, :]
        nc_ref[bb, CONV_K - 2:CONV_K - 1, :] = urow
        qrow = q_blk[bb:bb + 1, :]
        qm = jnp.concatenate(
            [jnp.where(lane_kh == kh, qrow[:, g * D_KV:(g + 1) * D_KV], 0.0)
             for g in range(GROUP) for kh in range(N_KV)], axis=0)
        kc = ck_ref[bb]
        vc = cv_ref[bb]
        knew = k_blk[bb:bb + 1, :]
        vnew = v_blk[bb:bb + 1, :]
        s = _dot_nt(qm.astype(BF16), kc.astype(BF16))
        snew = jnp.sum(qm * knew, axis=-1, keepdims=True)
        m = jnp.maximum(jnp.maximum(jnp.max(s, axis=-1, keepdims=True), snew), sink)
        e = jnp.exp2(s - m)
        en = jnp.exp2(snew - m)
        inv = 1.0 / (jnp.sum(e, axis=-1, keepdims=True) + en + jnp.exp2(sink - m))
        o = _dot((e * inv).astype(BF16), vc.astype(BF16)) + (en * inv) * vnew
        slabs = []
        for g in range(GROUP):
            acc = None
            for kh in range(N_KV):
                r = g * N_KV + kh
                t = jnp.where(lane_kh == kh, o[r:r + 1, :], 0.0)
                acc = t if acc is None else acc + t
            slabs.append(acc)
        arows.append(jnp.concatenate(slabs, axis=1))
        nk_ref[bb, 0:WINDOW - 1, :] = kc[1:WINDOW, :]
        nk_ref[bb, WINDOW - 1:WINDOW, :] = knew
        nv_ref[bb, 0:WINDOW - 1, :] = vc[1:WINDOW, :]
        nv_ref[bb, WINDOW - 1:WINDOW, :] = vnew
    c_s[pl.ds(b0, SB), :] = jnp.concatenate(crows, axis=0)
    a_s[pl.ds(b0, SB), :] = jnp.concatenate(arows, axis=0)

    @pl.when(i == n_steps - 1)
    def _():
        y = _layernorm(c_s[...], lng_ref[...], lnb_ref[...])
        y = y * _sigmoid(y)
        cbr = _dot(y.astype(BF16), wpw_ref[...])
        abr = _dot(a_s[...].astype(BF16), wao_ref[...])
        mix = gc_s[...] * cbr + ga_s[...] * abr
        xo_ref[...] = xs_ref[...] + _dot(mix.astype(BF16), wout_ref[...])


def _smix_call(l, xs, gmix, win, wq, dw, db, lng, lnb, wpw, sinkc, wao, wout, sconv, ck, cv):
    nb = xs.shape[0]
    state_spec = lambda *dims: pl.BlockSpec((None, SB) + dims, lambda i: (l, i) + (0,) * len(dims))
    in_specs = [
        _const_spec((nb, D_MODEL)),
        _layer_spec(l, (1, D_MODEL)),
        _layer_spec(l, (D_MODEL, O_END)),
        _layer_spec(l, (D_MODEL, D_MODEL)),
        _layer_spec(l, (CONV_K, D_MODEL)),
        _layer_spec(l, (1, D_MODEL)),
        _layer_spec(l, (1, D_MODEL)),
        _layer_spec(l, (1, D_MODEL)),
        _layer_spec(l, (D_MODEL, D_MODEL)),
        _layer_spec(l, (GROUP * N_KV, LANES)),
        _layer_spec(l, (D_MODEL, D_MODEL)),
        _layer_spec(l, (D_MODEL, D_MODEL)),
        state_spec(CONV_K - 1, D_MODEL),
        state_spec(WINDOW, D_KV),
        state_spec(WINDOW, D_KV),
    ]
    out_shape = (
        jax.ShapeDtypeStruct((nb, D_MODEL), F32),
        jax.ShapeDtypeStruct((nb, WINDOW, D_KV), F32),
        jax.ShapeDtypeStruct((nb, WINDOW, D_KV), F32),
        jax.ShapeDtypeStruct((nb, CONV_K - 1, D_MODEL), F32),
    )
    out_specs = (
        pl.BlockSpec((nb, D_MODEL), lambda i: (0, 0)),
        pl.BlockSpec((SB, WINDOW, D_KV), lambda i: (i, 0, 0)),
        pl.BlockSpec((SB, WINDOW, D_KV), lambda i: (i, 0, 0)),
        pl.BlockSpec((SB, CONV_K - 1, D_MODEL), lambda i: (i, 0, 0)),
    )
    scratch = [
        pltpu.VMEM((nb, D_MODEL), F32),
        pltpu.VMEM((nb, D_MODEL), F32),
        pltpu.VMEM((nb, D_KV), F32),
        pltpu.VMEM((nb, D_KV), F32),
        pltpu.VMEM((nb, D_MODEL), F32),
        pltpu.VMEM((nb, D_MODEL), F32),
        pltpu.VMEM((nb, D_MODEL), F32),
        pltpu.VMEM((nb, D_MODEL), F32),
    ]
    return pl.pallas_call(
        _smix_kernel,
        grid=(nb // SB,),
        in_specs=in_specs,
        out_specs=out_specs,
        out_shape=out_shape,
        scratch_shapes=scratch,
        compiler_params=pltpu.CompilerParams(
            dimension_semantics=("arbitrary",), vmem_limit_bytes=VMEM_LIMIT),
        name="sample_mixer",
    )(xs, gmix, win, wq, dw, db, lng, lnb, wpw, sinkc, wao, wout, sconv, ck, cv)


SQ = 4
S_STEPS = 32


def _smix2_kernel(n_alias, *refs):
    (xs_ref, gmix_ref, win_ref, wq_ref, wkvt_ref, dw_ref, db_ref, lng_ref, lnb_ref, wpw_ref,
     sinkc_ref, wao_ref, wout_ref, sc_ref, kt_ref, vt_ref) = refs[:16]
    (xo_ref, nkt_ref, nvt_ref, nct_ref,
     u_s, q_s, k_s, v_s, kvt_s, gc_s, ga_s, acc_s, a_s) = refs[16 + n_alias:]
    s = pl.program_id(0)
    n_hist = CONV_K - 1

    @pl.when(s == 0)
    def _():
        xn = _rmsnorm(xs_ref[...], gmix_ref[...]).astype(BF16)

        def proj(lo, hi):
            return _dot(xn, win_ref[:, lo:hi])

        u_s[...] = proj(O_UA, O_UB) * _sigmoid(proj(O_UB, O_Q))
        q_s[...] = _dot(xn, wq_ref[...])
        k_s[...] = proj(O_K, O_V)
        v_s[...] = proj(O_V, O_GC)
        kvt_s[...] = _dot_nt(wkvt_ref[...], xn)
        gc_s[...] = _sigmoid(proj(O_GC, O_GA))
        ga_s[...] = _sigmoid(proj(O_GA, O_END))
        acc_s[...] = jnp.zeros(acc_s.shape, F32)

    slab = sc_ref[...]

    @pl.when(s < n_hist)
    def _():
        acc_s[...] += slab * dw_ref[pl.ds(s, 1), :]

    nct_ref[...] = jnp.where(s < n_hist, slab, u_s[...])

    lane_kh = lax.broadcasted_iota(jnp.int32, (1, D_KV), 1) // HEAD_DIM
    lane_w = lax.broadcasted_iota(jnp.int32, (1, WINDOW), 1)
    sink = sinkc_ref[:, 0:1]
    for bb in range(SQ):
        b = s * SQ + bb
        qrow = q_s[pl.ds(b, 1), :]
        knew = k_s[pl.ds(b, 1), :]
        vnew = v_s[pl.ds(b, 1), :]
        kt = kt_ref[bb].reshape(D_KV, WINDOW)
        vt = vt_ref[bb].reshape(D_KV, WINDOW)
        qm = jnp.concatenate(
            [jnp.where(lane_kh == kh, qrow[:, g * D_KV:(g + 1) * D_KV], 0.0)
             for g in range(GROUP) for kh in range(N_KV)], axis=0)
        sc = _dot(qm.astype(BF16), kt.astype(BF16))
        snew = jnp.sum(qm * knew, axis=-1, keepdims=True)
        m = jnp.maximum(jnp.maximum(jnp.max(sc, axis=-1, keepdims=True), snew), sink)
        e = jnp.exp2(sc - m)
        en = jnp.exp2(snew - m)
        inv = 1.0 / (jnp.sum(e, axis=-1, keepdims=True) + en + jnp.exp2(sink - m))
        o = _dot_nt((e * inv).astype(BF16), vt.astype(BF16)) + (en * inv) * vnew
        slabs = []
        for g in range(GROUP):
            acc = None
            for kh in range(N_KV):
                r = g * N_KV + kh
                t = jnp.where(lane_kh == kh, o[r:r + 1, :], 0.0)
                acc = t if acc is None else acc + t
            slabs.append(acc)
        a_s[pl.ds(b, 1), :] = jnp.concatenate(slabs, axis=1)
        kcol = pltpu.roll(kvt_s[0:D_KV, :], WINDOW - 1 - b, axis=1)
        vcol = pltpu.roll(kvt_s[D_KV:2 * D_KV, :], WINDOW - 1 - b, axis=1)
        last = lane_w == WINDOW - 1
        nkt_ref[bb] = jnp.where(last, kcol, pltpu.roll(kt, WINDOW - 1, axis=1)).reshape(
            N_KV, HEAD_DIM, WINDOW)
        nvt_ref[bb] = jnp.where(last, vcol, pltpu.roll(vt, WINDOW - 1, axis=1)).reshape(
            N_KV, HEAD_DIM, WINDOW)

    @pl.when(s == S_STEPS - 1)
    def _():
        c = acc_s[...] + u_s[...] * dw_ref[n_hist:CONV_K, :] + db_ref[...]
        y = _layernorm(c, lng_ref[...], lnb_ref[...])
        y = y * _sigmoid(y)
        cbr = _dot(y.astype(BF16), wpw_ref[...])
        abr = _dot(a_s[...].astype(BF16), wao_ref[...])
        mix = gc_s[...] * cbr + ga_s[...] * abr
        xo_ref[...] = xs_ref[...] + _dot(mix.astype(BF16), wout_ref[...])


def _smix2_call(l, xs, gmix, win, wq, wkvt, dw, db, lng, lnb, wpw, sinkc, wao, wout, sct, kt, vt,
                prev):
    nb = xs.shape[0]
    depth = kt.shape[0]
    n_hist = CONV_K - 1
    assert nb == SQ * S_STEPS and nb == WINDOW and S_STEPS >= n_hist + 1
    kv_spec = pl.BlockSpec((None, SQ, N_KV, HEAD_DIM, WINDOW), lambda s: (l, s, 0, 0, 0))
    in_specs = [
        _const_spec((nb, D_MODEL)),
        _layer_spec(l, (1, D_MODEL)),
        _layer_spec(l, (D_MODEL, O_END)),
        _layer_spec(l, (D_MODEL, D_MODEL)),
        _layer_spec(l, (2 * D_KV, D_MODEL)),
        _layer_spec(l, (CONV_K, D_MODEL)),
        _layer_spec(l, (1, D_MODEL)),
        _layer_spec(l, (1, D_MODEL)),
        _layer_spec(l, (1, D_MODEL)),
        _layer_spec(l, (D_MODEL, D_MODEL)),
        _layer_spec(l, (GROUP * N_KV, LANES)),
        _layer_spec(l, (D_MODEL, D_MODEL)),
        _layer_spec(l, (D_MODEL, D_MODEL)),
        pl.BlockSpec((None, None, nb, D_MODEL), lambda s: (l, jnp.minimum(s, n_hist - 1), 0, 0)),
        kv_spec,
        kv_spec,
    ]
    args = [xs, gmix, win, wq, wkvt, dw, db, lng, lnb, wpw, sinkc, wao, wout, sct, kt, vt]
    aliases = {}
    if prev is not None:
        in_specs += [pl.BlockSpec(memory_space=pl.ANY)] * len(prev)
        aliases = {len(args) + k: 1 + k for k in range(len(prev))}
        args += list(prev)
    out_shape = (
        jax.ShapeDtypeStruct((nb, D_MODEL), F32),
        jax.ShapeDtypeStruct((depth, nb, N_KV, HEAD_DIM, WINDOW), F32),
        jax.ShapeDtypeStruct((depth, nb, N_KV, HEAD_DIM, WINDOW), F32),
        jax.ShapeDtypeStruct((depth, n_hist, nb, D_MODEL), F32),
    )
    out_specs = (
        pl.BlockSpec((nb, D_MODEL), lambda s: (0, 0)),
        kv_spec,
        kv_spec,
        pl.BlockSpec((None, None, nb, D_MODEL),
                     lambda s: (l, jnp.clip(s - 1, 0, n_hist - 1), 0, 0)),
    )
    scratch = [
        pltpu.VMEM((nb, D_MODEL), F32),
        pltpu.VMEM((nb, D_MODEL), F32),
        pltpu.VMEM((nb, D_KV), F32),
        pltpu.VMEM((nb, D_KV), F32),
        pltpu.VMEM((2 * D_KV, nb), F32),
        pltpu.VMEM((nb, D_MODEL), F32),
        pltpu.VMEM((nb, D_MODEL), F32),
        pltpu.VMEM((nb, D_MODEL), F32),
        pltpu.VMEM((nb, D_MODEL), F32),
    ]
    return pl.pallas_call(
        functools.partial(_smix2_kernel, 0 if prev is None else len(prev)),
        grid=(S_STEPS,),
        in_specs=in_specs,
        out_specs=out_specs,
        out_shape=out_shape,
        scratch_shapes=scratch,
        input_output_aliases=aliases,
        compiler_params=pltpu.CompilerParams(
            dimension_semantics=("arbitrary",), vmem_limit_bytes=VMEM_LIMIT),
        name="sample_mixer",
    )(*args)


def _sffn_kernel(final, *refs):
    if final:
        (x_ref, gffn_ref, wup_ref, s0_ref, s1_ref, fdw_ref, fdb_ref, wdn_ref, gfin_ref,
         xo_ref, h_ref) = refs
    else:
        (x_ref, gffn_ref, wup_ref, s0_ref, s1_ref, fdw_ref, fdb_ref, wdn_ref,
         xo_ref, h_ref) = refs
    x = x_ref[...]
    xn = _rmsnorm(x, gffn_ref[...]).astype(BF16)
    h = _dot(xn, wup_ref[:, 0:D_FF])
    gate = _dot(xn, wup_ref[:, D_FF:2 * D_FF])
    h_ref[...] = h
    cc = (s0_ref[...] * fdw_ref[0:1, :] + s1_ref[...] * fdw_ref[1:2, :]
          + h * fdw_ref[2:3, :] + fdb_ref[...])
    y = x + _dot((_gelu_tanh(cc) * gate).astype(BF16), wdn_ref[...])
    if final:
        y = _rmsnorm(y, gfin_ref[...])
    xo_ref[...] = y


def _sffn_call(l, final, x, gffn, wup, s0, s1, fdw, fdb, wdn, gfin):
    nb = x.shape[0]
    in_specs = [
        _const_spec((nb, D_MODEL)),
        _layer_spec(l, (1, D_MODEL)),
        _layer_spec(l, (D_MODEL, 2 * D_FF)),
        _const_spec((nb, D_FF)),
        _const_spec((nb, D_FF)),
        _layer_spec(l, (FFN_K, D_FF)),
        _layer_spec(l, (1, D_FF)),
        _layer_spec(l, (D_FF, D_MODEL)),
    ]
    args = [x, gffn, wup, s0, s1, fdw, fdb, wdn]
    if final:
        in_specs.append(_const_spec((1, D_MODEL)))
        args.append(gfin)
    return pl.pallas_call(
        functools.partial(_sffn_kernel, final),
        grid=(1,),
        in_specs=in_specs,
        out_specs=(pl.BlockSpec((nb, D_MODEL), lambda i: (0, 0)),
                   pl.BlockSpec((nb, D_FF), lambda i: (0, 0))),
        out_shape=(jax.ShapeDtypeStruct((nb, D_MODEL), F32),
                   jax.ShapeDtypeStruct((nb, D_FF), F32)),
        compiler_params=pltpu.CompilerParams(
            dimension_semantics=("arbitrary",), vmem_limit_bytes=VMEM_LIMIT),
        name="sample_ffn",
    )(*args)


def _group_major(w, axis):
    shape = w.shape
    w = w.reshape(shape[:axis] + (N_KV, GROUP, HEAD_DIM) + shape[axis + 1:])
    w = jnp.swapaxes(w, axis, axis + 1)
    return w.reshape(shape)


def kernel(x_prompt, x_sample, cache_swa_k, cache_swa_v, state_conv, state_ffn_conv, meta_tokens, norm_mix, w_in, conv_dw, conv_db, conv_ln_g, conv_ln_b, w_conv_pw, attn_sinks, w_attn_o, w_out, norm_ffn, w_ffn_up, ffn_dw, ffn_db, w_ffn_down, norm_final):
    depth = w_in.shape[0]
    batch, seq, _ = x_prompt.shape
    assert batch == 1 and seq % TILE == 0
    n_tiles = 1 + seq // TILE
    nb = x_sample.shape[0]

    win_b = w_in.astype(BF16)
    wq_b = (_group_major(w_in[:, :, O_Q:O_K], 2) * (ATT_SCALE * LOG2E)).astype(BF16)
    wpw_b = w_conv_pw.astype(BF16)
    wao_b = _group_major(w_attn_o, 1).astype(BF16)
    wout_b = w_out.astype(BF16)
    wup_b = w_ffn_up.astype(BF16)
    wdn_b = w_ffn_down.astype(BF16)
    sinks_gk = jnp.swapaxes(attn_sinks.reshape(depth, N_KV, GROUP), 1, 2).reshape(depth, GROUP * N_KV)
    sinks_gk = sinks_gk * LOG2E
    sinks_col = jnp.broadcast_to(sinks_gk[:, :, None], (depth, GROUP * N_KV, LANES))
    dwb = conv_dw.reshape(depth, CONV_K, N_CCH, LANES).transpose(0, 2, 1, 3)
    dwb = jnp.repeat(dwb, SUBLANES, axis=2)
    row = lambda a: a[:, None, :]
    gmix, db, lng, lnb, gffn, fdb = map(row, (norm_mix, conv_db, conv_ln_g, conv_ln_b, norm_ffn, ffn_db))
    gfin = norm_final[None, :]

    x_first = jnp.concatenate([jnp.zeros((PAD, D_MODEL), F32), meta_tokens.astype(F32)], axis=0)
    xp = x_prompt[0]
    xs = x_sample[:, 0, :]
    kt_all = jnp.transpose(cache_swa_k, (0, 1, 3, 4, 2))
    vt_all = jnp.transpose(cache_swa_v, (0, 1, 3, 4, 2))
    sc_t = jnp.transpose(state_conv, (0, 2, 1, 3))
    wkvt_b = jnp.swapaxes(w_in[:, :, O_K:O_GC], 1, 2).astype(BF16)
    sample_state = None

    kp, vp, cp, fp, fs = [], [], [], [], []
    for l in range(depth):
        final = l == depth - 1
        xm, knew, vnew, unew = _pmix_call(
            l, n_tiles, x_first, xp, gmix, win_b, wq_b, dwb, db, lng, lnb, wpw_b, sinks_gk[l],
            wao_b, wout_b)
        xp, hnew = _pffn_call(l, final, n_tiles, xm, gffn, wup_b, ffn_dw, fdb, wdn_b, gfin)
        kp.append(knew)
        vp.append(vnew)
        cp.append(unew[CONV_HIST - (CONV_K - 1):])
        fp.append(hnew[FFN_HIST - (FFN_K - 1):])

        xsm, *sample_state = _smix2_call(
            l, xs, gmix, win_b, wq_b, wkvt_b, conv_dw, db, lng, lnb, wpw_b, sinks_col, wao_b,
            wout_b, sc_t, kt_all, vt_all, sample_state)
        s0 = state_ffn_conv[l, :, 0, :]
        s1 = state_ffn_conv[l, :, 1, :]
        xs, hs = _sffn_call(l, final, xsm, gffn, wup_b, s0, s1, ffn_dw, fdb, wdn_b, gfin)
        fs.append(jnp.stack([s1, hs], axis=1))

    nkt, nvt, nct = sample_state
    kv5 = lambda a, b: jnp.stack(a).reshape(depth, b, WINDOW, N_KV, HEAD_DIM)
    return (xp[None], xs[:, None, :],
            kv5(kp, 1), kv5(vp, 1), jnp.stack(cp)[:, None], jnp.stack(fp)[:, None],
            jnp.transpose(nkt, (0, 1, 4, 2, 3)), jnp.transpose(nvt, (0, 1, 4, 2, 3)),
            jnp.transpose(nct, (0, 2, 1, 3)), jnp.stack(fs))
```

```python
import functools
import math

import jax
import jax.numpy as jnp
from jax import lax
from jax.experimental import pallas as pl
from jax.experimental.pallas import tpu as pltpu

F32 = jnp.float32
BF16 = jnp.bfloat16

D_MODEL = 1024
N_META = 16
HEAD_DIM = 64
N_KV = 4
GROUP = 4
D_KV = N_KV * HEAD_DIM
WINDOW = 128
CONV_K = 31
D_FF = 2816
FFN_K = 3
EPS = 1e-6
NEG = -1e30
ATT_SCALE = 1.0 / math.sqrt(HEAD_DIM)
LOG2E = math.log2(math.e)

LANES = 128
SUBLANES = 8
N_CCH = D_MODEL // LANES
N_FCH = D_FF // LANES

TILE = 512
PAD = TILE - N_META
CONV_HIST = 32
CCHUNK = 256
CONV_RB = 64
FFN_HIST = 8
ABLK = 128

O_UA, O_UB, O_Q, O_K, O_V, O_GC, O_GA, O_END = 0, 1024, 2048, 3072, 3328, 3584, 4608, 5632

VMEM_LIMIT = 60 * 1024 * 1024


def _sigmoid(x):
    return 1.0 / (1.0 + jnp.exp(-x))


def _gelu_tanh(x):
    return 0.5 * x * (1.0 + jnp.tanh(0.7978845608028654 * (x + 0.044715 * (x * x * x))))


def _rmsnorm(x, g):
    ms = jnp.mean(x * x, axis=-1, keepdims=True)
    return x * lax.rsqrt(ms + EPS) * g


def _layernorm(x, g, b):
    mu = jnp.mean(x, axis=-1, keepdims=True)
    d = x - mu
    var = jnp.mean(d * d, axis=-1, keepdims=True)
    return d * lax.rsqrt(var + EPS) * g + b


def _dot(a, b):
    return jnp.dot(a, b, preferred_element_type=F32)


def _dot_nt(a, b):
    return lax.dot_general(a, b, (((1,), (1,)), ((), ())), preferred_element_type=F32)


def _conv_lane_chunk(ubuf, dwb_ref, dbs_ref, c_s, c):
    first_start = CONV_HIST - (CONV_K - 1)
    bias = jnp.broadcast_to(dbs_ref[c], (SUBLANES, LANES))
    for base in range(0, TILE, CONV_RB):
        groups = range(base, base + CONV_RB, SUBLANES)
        acc = {}
        for phase in range(SUBLANES):
            ws = {j: dwb_ref[c, j * SUBLANES:(j + 1) * SUBLANES, :]
                  for j in range(phase, CONV_K, SUBLANES)}
            starts = sorted({first_start + r0 + j for r0 in groups for j in ws})
            for start in starts:
                win = ubuf[c, start:start + SUBLANES, :]
                for r0 in groups:
                    j = start - first_start - r0
                    if j in ws:
                        t = win * ws[j]
                        key = (r0, phase % 2)
                        acc[key] = t if key not in acc else acc[key] + t
        for r0 in groups:
            c_s[c, r0:r0 + SUBLANES, :] = (acc[(r0, 0)] + bias) + acc[(r0, 1)]


def _pmix_kernel(first, *refs):
    if first:
        xf_ref, refs = refs[0], refs[1:]
    (xm_ref, gmix_ref, win_ref, wq_ref, dwb_ref, dbs_ref, lng_ref, lnb_ref, wpw_ref, sinks_ref,
     wao_ref, wout_ref,
     xo_ref, knew_ref, vnew_ref, unew_ref,
     ubuf, uh_s, c_s, q_s, k_s, v_s, kh_s, vh_s, a_s) = refs
    T = TILE
    i = pl.program_id(0)
    slot = i % 2
    dyn_zero = jnp.minimum(i, 0)

    @pl.when(i == 0)
    def _():
        uh_s[0] = jnp.zeros((N_CCH, CONV_HIST, LANES), F32)
        kh_s[0] = jnp.zeros((N_KV, WINDOW, D_KV), BF16)
        vh_s[0] = jnp.zeros((N_KV, WINDOW, D_KV), BF16)

    ubuf[:, 0:CONV_HIST, :] = uh_s[slot]

    x = xm_ref[...]
    if first:
        x = jnp.where(i == 0, xf_ref[...], x)
    xn = _rmsnorm(x, gmix_ref[...]).astype(BF16)

    def proj(lo, hi):
        return _dot(xn, win_ref[:, lo:hi])

    rows = i * T + lax.broadcasted_iota(jnp.int32, (T, 1), 0)
    real = rows >= PAD
    lane_kh = lax.broadcasted_iota(jnp.int32, (1, D_KV), 1) // HEAD_DIM
    lanes_per_cw = CCHUNK // LANES

    def glu(cw):
        lo = cw * CCHUNK
        u = proj(O_UA + lo, O_UA + lo + CCHUNK) * _sigmoid(proj(O_UB + lo, O_UB + lo + CCHUNK))
        u = jnp.where(real, u, 0.0)
        unew_ref[:, lo:lo + CCHUNK] = u[T - CONV_HIST:, :]
        for cc in range(lanes_per_cw):
            uc = u[:, cc * LANES:(cc + 1) * LANES]
            ubuf[cw * lanes_per_cw + cc, CONV_HIST:CONV_HIST + T, :] = uc
            uh_s[1 - slot, cw * lanes_per_cw + cc] = uc[T - CONV_HIST:, :]

    def conv(cw):
        for cc in range(lanes_per_cw):
            _conv_lane_chunk(ubuf, dwb_ref, dbs_ref, c_s, cw * lanes_per_cw + cc + dyn_zero)

    def proj_q(lo, hi):
        q_s[:, lo:hi] = _dot(xn, wq_ref[:, lo:hi]).astype(BF16)

    def proj_kv():
        kv = proj(O_K, O_GC)
        k = kv[:, 0:D_KV]
        v = kv[:, D_KV:2 * D_KV]
        knew_ref[...] = k[T - WINDOW:, :]
        vnew_ref[...] = v[T - WINDOW:, :]
        for kh in range(N_KV):
            sel = lane_kh == kh
            km = jnp.where(sel, k, 0.0).astype(BF16)
            vm = jnp.where(sel, v, 0.0).astype(BF16)
            k_s[kh] = km
            v_s[kh] = vm
            kh_s[1 - slot, kh] = km[T - WINDOW:, :]
            vh_s[1 - slot, kh] = vm[T - WINDOW:, :]

    glu(0)
    glu(1)
    proj_q(0, 2 * D_KV)
    conv(0)
    glu(2)
    proj_q(2 * D_KV, 4 * D_KV)
    conv(1)
    glu(3)
    proj_kv()
    conv(2)
    sgc = _sigmoid(proj(O_GC, O_GA))
    conv(3)
    sga = _sigmoid(proj(O_GA, O_END))

    qi = lax.broadcasted_iota(jnp.int32, (ABLK, 2 * ABLK), 0)
    kj = lax.broadcasted_iota(jnp.int32, (ABLK, 2 * ABLK), 1)
    band = (kj >= qi) & (kj <= qi + WINDOW)

    def window(cur, hist, kh, r0):
        if r0 == 0:
            return jnp.concatenate([hist[slot, kh], cur[kh, 0:ABLK, :]], axis=0)
        return cur[kh, r0 - ABLK:r0 + ABLK, :]

    def attn_block(r0):
        ok = band & (i * T + r0 - WINDOW + kj >= PAD)
        qb = q_s[r0:r0 + ABLK, :]
        qall = jnp.concatenate([qb[:, g * D_KV:(g + 1) * D_KV] for g in range(GROUP)], axis=0)
        probs = [[None] * N_KV for _ in range(GROUP)]
        for kh in range(N_KV):
            s = _dot_nt(qall, window(k_s, kh_s, kh, r0))
            for g in range(GROUP):
                sink = sinks_ref[g * N_KV + kh]
                sh = jnp.where(ok, s[g * ABLK:(g + 1) * ABLK, :], NEG)
                m = jnp.maximum(jnp.max(sh, axis=-1, keepdims=True), sink)
                e = jnp.exp2(sh - m)
                l = jnp.sum(e, axis=-1, keepdims=True) + jnp.exp2(sink - m)
                probs[g][kh] = (e * (1.0 / l)).astype(BF16)
        vst = jnp.concatenate([window(v_s, vh_s, kh, r0) for kh in range(N_KV)], axis=0)
        for g in range(GROUP):
            o = _dot(jnp.concatenate(probs[g], axis=1), vst)
            a_s[r0:r0 + ABLK, g * D_KV:(g + 1) * D_KV] = o.astype(BF16)

    attn_block(0)
    cfull = jnp.concatenate([c_s[c] for c in range(N_CCH)], axis=1)
    y = _layernorm(cfull, lng_ref[...], lnb_ref[...])
    y = (y * _sigmoid(y)).astype(BF16)
    n_blk = T // ABLK
    pw_cols = D_MODEL // n_blk
    cbr = []
    for b in range(n_blk):
        cbr.append(_dot(y, wpw_ref[:, b * pw_cols:(b + 1) * pw_cols]))
        if b + 1 < n_blk:
            attn_block((b + 1) * ABLK)
    cbr = jnp.concatenate(cbr, axis=1)
    abr = _dot(a_s[...], wao_ref[...])

    mix = sgc * cbr + sga * abr
    xo_ref[...] = x + _dot(mix.astype(BF16), wout_ref[...])


def _layer_spec(l, shape):
    zeros = (0,) * len(shape)
    return pl.BlockSpec((None,) + tuple(shape), lambda i: (l,) + zeros,
                        pipeline_mode=pl.Buffered(1))


def _const_spec(shape):
    zeros = (0,) * len(shape)
    return pl.BlockSpec(shape, lambda i: zeros, pipeline_mode=pl.Buffered(1))


def _pmix_call(l, n_tiles, x_first, x_main, gmix, win, wq, dwb, dbs, lng, lnb, wpw, sinks, wao,
               wout):
    T = TILE
    first = l == 0
    if first:
        x_spec = pl.BlockSpec((T, D_MODEL), lambda i: (jnp.maximum(i - 1, 0), 0))
    else:
        x_spec = pl.BlockSpec((T, D_MODEL), lambda i: (i, 0))
    in_specs = [
        x_spec,
        _layer_spec(l, (1, D_MODEL)),
        _layer_spec(l, (D_MODEL, O_END)),
        _layer_spec(l, (D_MODEL, D_MODEL)),
        _layer_spec(l, (N_CCH, CONV_K * SUBLANES, LANES)),
        _layer_spec(l, (N_CCH, 1, LANES)),
        _layer_spec(l, (1, D_MODEL)),
        _layer_spec(l, (1, D_MODEL)),
        _layer_spec(l, (D_MODEL, D_MODEL)),
        pl.BlockSpec(memory_space=pltpu.SMEM),
        _layer_spec(l, (D_MODEL, D_MODEL)),
        _layer_spec(l, (D_MODEL, D_MODEL)),
    ]
    args = [x_main, gmix, win, wq, dwb, dbs, lng, lnb, wpw, sinks, wao, wout]
    if first:
        in_specs = [_const_spec((T, D_MODEL))] + in_specs
        args = [x_first] + args
    out_shape = (
        jax.ShapeDtypeStruct((n_tiles * T, D_MODEL), F32),
        jax.ShapeDtypeStruct((WINDOW, D_KV), F32),
        jax.ShapeDtypeStruct((WINDOW, D_KV), F32),
        jax.ShapeDtypeStruct((CONV_HIST, D_MODEL), F32),
    )
    out_specs = (
        pl.BlockSpec((T, D_MODEL), lambda i: (i, 0)),
        pl.BlockSpec((WINDOW, D_KV), lambda i: (0, 0)),
        pl.BlockSpec((WINDOW, D_KV), lambda i: (0, 0)),
        pl.BlockSpec((CONV_HIST, D_MODEL), lambda i: (0, 0)),
    )
    scratch = [
        pltpu.VMEM((N_CCH, CONV_HIST + T, LANES), F32),
        pltpu.VMEM((2, N_CCH, CONV_HIST, LANES), F32),
        pltpu.VMEM((N_CCH, T, LANES), F32),
        pltpu.VMEM((T, D_MODEL), BF16),
        pltpu.VMEM((N_KV, T, D_KV), BF16),
        pltpu.VMEM((N_KV, T, D_KV), BF16),
        pltpu.VMEM((2, N_KV, WINDOW, D_KV), BF16),
        pltpu.VMEM((2, N_KV, WINDOW, D_KV), BF16),
        pltpu.VMEM((T, D_MODEL), BF16),
    ]
    return pl.pallas_call(
        functools.partial(_pmix_kernel, first),
        grid=(n_tiles,),
        in_specs=in_specs,
        out_specs=out_specs,
        out_shape=out_shape,
        scratch_shapes=scratch,
        compiler_params=pltpu.CompilerParams(
            dimension_semantics=("arbitrary",), vmem_limit_bytes=VMEM_LIMIT),
        name="prompt_mixer",
    )(*args)


def _pffn_kernel(final, *refs):
    if final:
        (xm_ref, gffn_ref, wup_ref, fdw_ref, fdb_ref, wdn_ref, gfin_ref,
         xo_ref, hnew_ref, hbuf) = refs
    else:
        (xm_ref, gffn_ref, wup_ref, fdw_ref, fdb_ref, wdn_ref,
         xo_ref, hnew_ref, hbuf) = refs
    T = TILE
    i = pl.program_id(0)

    @pl.when(i == 0)
    def _():
        hbuf[:, 0:FFN_HIST, :] = jnp.zeros((N_FCH, FFN_HIST, LANES), F32)

    x = xm_ref[...]
    xn = _rmsnorm(x, gffn_ref[...]).astype(BF16)
    h = _dot(xn, wup_ref[:, 0:D_FF])
    gate = _dot(xn, wup_ref[:, D_FF:2 * D_FF])
    rows = i * T + lax.broadcasted_iota(jnp.int32, (T, 1), 0)
    h = jnp.where(rows >= PAD, h, 0.0)
    hnew_ref[...] = h[T - FFN_HIST:, :]
    acts = []
    for c in range(N_FCH):
        sl = slice(c * LANES, (c + 1) * LANES)
        hc = h[:, sl]
        hbuf[c, FFN_HIST:FFN_HIST + T, :] = hc
        cc = (hbuf[c, FFN_HIST - 2:FFN_HIST - 2 + T, :] * fdw_ref[0:1, sl]
              + hbuf[c, FFN_HIST - 1:FFN_HIST - 1 + T, :] * fdw_ref[1:2, sl]
              + hc * fdw_ref[2:3, sl] + fdb_ref[:, sl])
        acts.append((_gelu_tanh(cc) * gate[:, sl]).astype(BF16))
        hbuf[c, 0:FFN_HIST, :] = hbuf[c, T:T + FFN_HIST, :]
    y = x + _dot(jnp.concatenate(acts, axis=1), wdn_ref[...])
    if final:
        y = _rmsnorm(y, gfin_ref[...])
    xo_ref[...] = y


def _pffn_call(l, final, n_tiles, xm, gffn, wup, fdw, fdb, wdn, gfin):
    T = TILE
    in_specs = [
        pl.BlockSpec((T, D_MODEL), lambda i: (i, 0)),
        _layer_spec(l, (1, D_MODEL)),
        _layer_spec(l, (D_MODEL, 2 * D_FF)),
        _layer_spec(l, (FFN_K, D_FF)),
        _layer_spec(l, (1, D_FF)),
        _layer_spec(l, (D_FF, D_MODEL)),
    ]
    args = [xm, gffn, wup, fdw, fdb, wdn]
    if final:
        in_specs.append(_const_spec((1, D_MODEL)))
        args.append(gfin)
        x_out = jax.ShapeDtypeStruct(((n_tiles - 1) * T, D_MODEL), F32)
        x_out_spec = pl.BlockSpec((T, D_MODEL), lambda i: (jnp.maximum(i - 1, 0), 0))
    else:
        x_out = jax.ShapeDtypeStruct((n_tiles * T, D_MODEL), F32)
        x_out_spec = pl.BlockSpec((T, D_MODEL), lambda i: (i, 0))
    return pl.pallas_call(
        functools.partial(_pffn_kernel, final),
        grid=(n_tiles,),
        in_specs=in_specs,
        out_specs=(x_out_spec, pl.BlockSpec((FFN_HIST, D_FF), lambda i: (0, 0))),
        out_shape=(x_out, jax.ShapeDtypeStruct((FFN_HIST, D_FF), F32)),
        scratch_shapes=[pltpu.VMEM((N_FCH, FFN_HIST + T, LANES), F32)],
        compiler_params=pltpu.CompilerParams(
            dimension_semantics=("arbitrary",), vmem_limit_bytes=VMEM_LIMIT),
        name="prompt_ffn",
    )(*args)


SQ = 4
S_STEPS = 32
N_SMIX_IN = 15


def _smix_kernel(n_alias, *refs):
    (xs_ref, gmix_ref, win_ref, wq_ref, dw_ref, db_ref, lng_ref, lnb_ref, wpw_ref,
     sinkc_ref, wao_ref, wout_ref, sc_ref, kt_ref, vt_ref) = refs[:N_SMIX_IN]
    (xo_ref, nkt_ref, nvt_ref, nct_ref,
     u_s, q_s, k_s, v_s, kvt_s, gc_s, ga_s, acc_s, a_s) = refs[N_SMIX_IN + n_alias:]
    s = pl.program_id(0)
    n_hist = CONV_K - 1

    @pl.when(s == 0)
    def _():
        xn = _rmsnorm(xs_ref[...], gmix_ref[...]).astype(BF16)

        def proj(lo, hi):
            return _dot(xn, win_ref[:, lo:hi])

        u_s[...] = proj(O_UA, O_UB) * _sigmoid(proj(O_UB, O_Q))
        q_s[...] = _dot(xn, wq_ref[...])
        kv = proj(O_K, O_GC)
        k_s[...] = kv[:, 0:D_KV]
        v_s[...] = kv[:, D_KV:2 * D_KV]
        kvt_s[...] = kv.T
        gc_s[...] = _sigmoid(proj(O_GC, O_GA))
        ga_s[...] = _sigmoid(proj(O_GA, O_END))
        acc_s[...] = jnp.zeros(acc_s.shape, F32)

    slab = sc_ref[...]

    @pl.when(s < n_hist)
    def _():
        acc_s[...] += slab * dw_ref[pl.ds(s, 1), :]

    nct_ref[...] = jnp.where(s < n_hist, slab, u_s[...])

    lane_kh = lax.broadcasted_iota(jnp.int32, (1, D_KV), 1) // HEAD_DIM
    lane_w = lax.broadcasted_iota(jnp.int32, (1, WINDOW), 1)
    sink = sinkc_ref[:, 0:1]
    for bb in range(SQ):
        b = s * SQ + bb
        qrow = q_s[pl.ds(b, 1), :]
        knew = k_s[pl.ds(b, 1), :]
        vnew = v_s[pl.ds(b, 1), :]
        kt = kt_ref[bb].reshape(D_KV, WINDOW)
        vt = vt_ref[bb].reshape(D_KV, WINDOW)
        qm = jnp.concatenate(
            [jnp.where(lane_kh == kh, qrow[:, g * D_KV:(g + 1) * D_KV], 0.0)
             for g in range(GROUP) for kh in range(N_KV)], axis=0)
        sc = _dot(qm.astype(BF16), kt.astype(BF16))
        snew = jnp.sum(qm * knew, axis=-1, keepdims=True)
        m = jnp.maximum(jnp.maximum(jnp.max(sc, axis=-1, keepdims=True), snew), sink)
        e = jnp.exp2(sc - m)
        en = jnp.exp2(snew - m)
        inv = 1.0 / (jnp.sum(e, axis=-1, keepdims=True) + en + jnp.exp2(sink - m))
        o = _dot_nt((e * inv).astype(BF16), vt.astype(BF16)) + (en * inv) * vnew
        slabs = []
        for g in range(GROUP):
            acc = None
            for kh in range(N_KV):
                r = g * N_KV + kh
                t = jnp.where(lane_kh == kh, o[r:r + 1, :], 0.0)
                acc = t if acc is None else acc + t
            slabs.append(acc)
        a_s[pl.ds(b, 1), :] = jnp.concatenate(slabs, axis=1)
        kcol = pltpu.roll(kvt_s[0:D_KV, :], WINDOW - 1 - b, axis=1)
        vcol = pltpu.roll(kvt_s[D_KV:2 * D_KV, :], WINDOW - 1 - b, axis=1)
        last = lane_w == WINDOW - 1
        nkt_ref[bb] = jnp.where(last, kcol, pltpu.roll(kt, WINDOW - 1, axis=1)).reshape(
            N_KV, HEAD_DIM, WINDOW)
        nvt_ref[bb] = jnp.where(last, vcol, pltpu.roll(vt, WINDOW - 1, axis=1)).reshape(
            N_KV, HEAD_DIM, WINDOW)

    @pl.when(s == S_STEPS - 1)
    def _():
        c = acc_s[...] + u_s[...] * dw_ref[n_hist:CONV_K, :] + db_ref[...]
        y = _layernorm(c, lng_ref[...], lnb_ref[...])
        y = y * _sigmoid(y)
        cbr = _dot(y.astype(BF16), wpw_ref[...])
        abr = _dot(a_s[...].astype(BF16), wao_ref[...])
        mix = gc_s[...] * cbr + ga_s[...] * abr
        xo_ref[...] = xs_ref[...] + _dot(mix.astype(BF16), wout_ref[...])


def _smix_call(l, xs, gmix, win, wq, dw, db, lng, lnb, wpw, sinkc, wao, wout, sct, kt, vt, prev):
    nb = xs.shape[0]
    depth = kt.shape[0]
    n_hist = CONV_K - 1
    assert nb == SQ * S_STEPS and nb == WINDOW and S_STEPS >= n_hist + 1
    kv_spec = pl.BlockSpec((None, SQ, N_KV, HEAD_DIM, WINDOW), lambda s: (l, s, 0, 0, 0))
    in_specs = [
        _const_spec((nb, D_MODEL)),
        _layer_spec(l, (1, D_MODEL)),
        _layer_spec(l, (D_MODEL, O_END)),
        _layer_spec(l, (D_MODEL, D_MODEL)),
        _layer_spec(l, (CONV_K, D_MODEL)),
        _layer_spec(l, (1, D_MODEL)),
        _layer_spec(l, (1, D_MODEL)),
        _layer_spec(l, (1, D_MODEL)),
        _layer_spec(l, (D_MODEL, D_MODEL)),
        _layer_spec(l, (GROUP * N_KV, LANES)),
        _layer_spec(l, (D_MODEL, D_MODEL)),
        _layer_spec(l, (D_MODEL, D_MODEL)),
        pl.BlockSpec((None, None, nb, D_MODEL), lambda s: (l, jnp.minimum(s, n_hist - 1), 0, 0)),
        kv_spec,
        kv_spec,
    ]
    args = [xs, gmix, win, wq, dw, db, lng, lnb, wpw, sinkc, wao, wout, sct, kt, vt]
    assert len(args) == N_SMIX_IN
    aliases = {}
    if prev is not None:
        in_specs += [pl.BlockSpec(memory_space=pl.ANY)] * len(prev)
        aliases = {len(args) + k: 1 + k for k in range(len(prev))}
        args += list(prev)
    out_shape = (
        jax.ShapeDtypeStruct((nb, D_MODEL), F32),
        jax.ShapeDtypeStruct((depth, nb, N_KV, HEAD_DIM, WINDOW), F32),
        jax.ShapeDtypeStruct((depth, nb, N_KV, HEAD_DIM, WINDOW), F32),
        jax.ShapeDtypeStruct((depth, n_hist, nb, D_MODEL), F32),
    )
    out_specs = (
        pl.BlockSpec((nb, D_MODEL), lambda s: (0, 0)),
        kv_spec,
        kv_spec,
        pl.BlockSpec((None, None, nb, D_MODEL),
                     lambda s: (l, jnp.clip(s - 1, 0, n_hist - 1), 0, 0)),
    )
    scratch = [
        pltpu.VMEM((nb, D_MODEL), F32),
        pltpu.VMEM((nb, D_MODEL), F32),
        pltpu.VMEM((nb, D_KV), F32),
        pltpu.VMEM((nb, D_KV), F32),
        pltpu.VMEM((2 * D_KV, nb), F32),
        pltpu.VMEM((nb, D_MODEL), F32),
        pltpu.VMEM((nb, D_MODEL), F32),
        pltpu.VMEM((nb, D_MODEL), F32),
        pltpu.VMEM((nb, D_MODEL), F32),
    ]
    return pl.pallas_call(
        functools.partial(_smix_kernel, 0 if prev is None else len(prev)),
        grid=(S_STEPS,),
        in_specs=in_specs,
        out_specs=out_specs,
        out_shape=out_shape,
        scratch_shapes=scratch,
        input_output_aliases=aliases,
        compiler_params=pltpu.CompilerParams(
            dimension_semantics=("arbitrary",), vmem_limit_bytes=VMEM_LIMIT),
        name="sample_mixer",
    )(*args)


def _sffn_kernel(final, *refs):
    if final:
        (x_ref, gffn_ref, wup_ref, s0_ref, s1_ref, fdw_ref, fdb_ref, wdn_ref, gfin_ref,
         xo_ref, h_ref) = refs
    else:
        (x_ref, gffn_ref, wup_ref, s0_ref, s1_ref, fdw_ref, fdb_ref, wdn_ref,
         xo_ref, h_ref) = refs
    x = x_ref[...]
    xn = _rmsnorm(x, gffn_ref[...]).astype(BF16)
    h = _dot(xn, wup_ref[:, 0:D_FF])
    gate = _dot(xn, wup_ref[:, D_FF:2 * D_FF])
    h_ref[...] = h
    cc = (s0_ref[...] * fdw_ref[0:1, :] + s1_ref[...] * fdw_ref[1:2, :]
          + h * fdw_ref[2:3, :] + fdb_ref[...])
    y = x + _dot((_gelu_tanh(cc) * gate).astype(BF16), wdn_ref[...])
    if final:
        y = _rmsnorm(y, gfin_ref[...])
    xo_ref[...] = y


def _sffn_call(l, final, x, gffn, wup, s0, s1, fdw, fdb, wdn, gfin):
    nb = x.shape[0]
    in_specs = [
        _const_spec((nb, D_MODEL)),
        _layer_spec(l, (1, D_MODEL)),
        _layer_spec(l, (D_MODEL, 2 * D_FF)),
        _const_spec((nb, D_FF)),
        _const_spec((nb, D_FF)),
        _layer_spec(l, (FFN_K, D_FF)),
        _layer_spec(l, (1, D_FF)),
        _layer_spec(l, (D_FF, D_MODEL)),
    ]
    args = [x, gffn, wup, s0, s1, fdw, fdb, wdn]
    if final:
        in_specs.append(_const_spec((1, D_MODEL)))
        args.append(gfin)
    return pl.pallas_call(
        functools.partial(_sffn_kernel, final),
        grid=(1,),
        in_specs=in_specs,
        out_specs=(pl.BlockSpec((nb, D_MODEL), lambda i: (0, 0)),
                   pl.BlockSpec((nb, D_FF), lambda i: (0, 0))),
        out_shape=(jax.ShapeDtypeStruct((nb, D_MODEL), F32),
                   jax.ShapeDtypeStruct((nb, D_FF), F32)),
        compiler_params=pltpu.CompilerParams(
            dimension_semantics=("arbitrary",), vmem_limit_bytes=VMEM_LIMIT),
        name="sample_ffn",
    )(*args)


def _group_major(w, axis):
    shape = w.shape
    w = w.reshape(shape[:axis] + (N_KV, GROUP, HEAD_DIM) + shape[axis + 1:])
    w = jnp.swapaxes(w, axis, axis + 1)
    return w.reshape(shape)


def kernel(x_prompt, x_sample, cache_swa_k, cache_swa_v, state_conv, state_ffn_conv, meta_tokens, norm_mix, w_in, conv_dw, conv_db, conv_ln_g, conv_ln_b, w_conv_pw, attn_sinks, w_attn_o, w_out, norm_ffn, w_ffn_up, ffn_dw, ffn_db, w_ffn_down, norm_final):
    depth = w_in.shape[0]
    batch, seq, _ = x_prompt.shape
    assert batch == 1 and seq % TILE == 0
    n_tiles = 1 + seq // TILE
    nb = x_sample.shape[0]

    win_b = w_in.astype(BF16)
    wq_b = (_group_major(w_in[:, :, O_Q:O_K], 2) * (ATT_SCALE * LOG2E)).astype(BF16)
    wpw_b = w_conv_pw.astype(BF16)
    wao_b = _group_major(w_attn_o, 1).astype(BF16)
    wout_b = w_out.astype(BF16)
    wup_b = w_ffn_up.astype(BF16)
    wdn_b = w_ffn_down.astype(BF16)
    sinks_gk = jnp.swapaxes(attn_sinks.reshape(depth, N_KV, GROUP), 1, 2).reshape(depth, GROUP * N_KV)
    sinks_gk = sinks_gk * LOG2E
    sinks_col = jnp.broadcast_to(sinks_gk[:, :, None], (depth, GROUP * N_KV, LANES))
    dwb = conv_dw.reshape(depth, CONV_K, N_CCH, LANES).transpose(0, 2, 1, 3)
    dwb = jnp.repeat(dwb, SUBLANES, axis=2)
    dbs = conv_db.reshape(depth, N_CCH, 1, LANES)
    row = lambda a: a[:, None, :]
    gmix, db, lng, lnb, gffn, fdb = map(row, (norm_mix, conv_db, conv_ln_g, conv_ln_b, norm_ffn, ffn_db))
    gfin = norm_final[None, :]

    x_first = jnp.concatenate([jnp.zeros((PAD, D_MODEL), F32), meta_tokens.astype(F32)], axis=0)
    xp = x_prompt[0]
    xs = x_sample[:, 0, :]
    kt_all = jnp.transpose(cache_swa_k, (0, 1, 3, 4, 2))
    vt_all = jnp.transpose(cache_swa_v, (0, 1, 3, 4, 2))
    sc_t = jnp.transpose(state_conv, (0, 2, 1, 3))
    sample_state = None

    kp, vp, cp, fp, fs = [], [], [], [], []
    for l in range(depth):
        final = l == depth - 1
        xm, knew, vnew, unew = _pmix_call(
            l, n_tiles, x_first, xp, gmix, win_b, wq_b, dwb, dbs, lng, lnb, wpw_b, sinks_gk[l],
            wao_b, wout_b)
        xp, hnew = _pffn_call(l, final, n_tiles, xm, gffn, wup_b, ffn_dw, fdb, wdn_b, gfin)
        kp.append(knew)
        vp.append(vnew)
        cp.append(unew[CONV_HIST - (CONV_K - 1):])
        fp.append(hnew[FFN_HIST - (FFN_K - 1):])

        xsm, *sample_state = _smix_call(
            l, xs, gmix, win_b, wq_b, conv_dw, db, lng, lnb, wpw_b, sinks_col, wao_b, wout_b,
            sc_t, kt_all, vt_all, sample_state)
        s0 = state_ffn_conv[l, :, 0, :]
        s1 = state_ffn_conv[l, :, 1, :]
        xs, hs = _sffn_call(l, final, xsm, gffn, wup_b, s0, s1, ffn_dw, fdb, wdn_b, gfin)
        fs.append(jnp.stack([s1, hs], axis=1))

    nkt, nvt, nct = sample_state
    kv5 = lambda a, b: jnp.stack(a).reshape(depth, b, WINDOW, N_KV, HEAD_DIM)
    return (xp[None], xs[:, None, :],
            kv5(kp, 1), kv5(vp, 1), jnp.stack(cp)[:, None], jnp.stack(fp)[:, None],
            jnp.transpose(nkt, (0, 1, 4, 2, 3)), jnp.transpose(nvt, (0, 1, 4, 2, 3)),
            jnp.transpose(nct, (0, 2, 1, 3)), jnp.stack(fs))
```

```python
import functools
import math

import jax
import jax.numpy as jnp
from jax import lax
from jax.experimental import pallas as pl
from jax.experimental.pallas import tpu as pltpu

F32 = jnp.float32
BF16 = jnp.bfloat16

D_MODEL = 1024
N_META = 16
HEAD_DIM = 64
N_KV = 4
GROUP = 4
D_KV = N_KV * HEAD_DIM
WINDOW = 128
CONV_K = 31
D_FF = 2816
FFN_K = 3
EPS = 1e-6
NEG = -1e30
ATT_SCALE = 1.0 / math.sqrt(HEAD_DIM)
LOG2E = math.log2(math.e)

LANES = 128
SUBLANES = 8
N_CCH = D_MODEL // LANES
N_FCH = D_FF // LANES

TILE = 512
PAD = TILE - N_META
CONV_HIST = 32
CCHUNK = 256
CONV_RB = 64
FFN_HIST = 8
ABLK = 128

O_UA, O_UB, O_Q, O_K, O_V, O_GC, O_GA, O_END = 0, 1024, 2048, 3072, 3328, 3584, 4608, 5632

VMEM_LIMIT = 60 * 1024 * 1024


def _sigmoid(x):
    return 1.0 / (1.0 + jnp.exp(-x))


def _gelu_tanh(x):
    return 0.5 * x * (1.0 + jnp.tanh(0.7978845608028654 * (x + 0.044715 * (x * x * x))))


def _rmsnorm(x, g):
    ms = jnp.mean(x * x, axis=-1, keepdims=True)
    return x * lax.rsqrt(ms + EPS) * g


def _layernorm(x, g, b):
    mu = jnp.mean(x, axis=-1, keepdims=True)
    d = x - mu
    var = jnp.mean(d * d, axis=-1, keepdims=True)
    return d * lax.rsqrt(var + EPS) * g + b


def _dot(a, b):
    return jnp.dot(a, b, preferred_element_type=F32)


def _dot_nt(a, b):
    return lax.dot_general(a, b, (((1,), (1,)), ((), ())), preferred_element_type=F32)


def _conv_lane_chunk(ubuf, dwb_ref, dbs_ref, c_s, c):
    first_start = CONV_HIST - (CONV_K - 1)
    bias = jnp.broadcast_to(dbs_ref[c], (SUBLANES, LANES))
    for base in range(0, TILE, CONV_RB):
        groups = range(base, base + CONV_RB, SUBLANES)
        acc = {}
        for phase in range(SUBLANES):
            ws = {j: dwb_ref[c, j * SUBLANES:(j + 1) * SUBLANES, :]
                  for j in range(phase, CONV_K, SUBLANES)}
            starts = sorted({first_start + r0 + j for r0 in groups for j in ws})
            for start in starts:
                win = ubuf[c, start:start + SUBLANES, :]
                for r0 in groups:
                    j = start - first_start - r0
                    if j in ws:
                        t = win * ws[j]
                        key = (r0, phase % 2)
                        acc[key] = t if key not in acc else acc[key] + t
        for r0 in groups:
            c_s[c, r0:r0 + SUBLANES, :] = (acc[(r0, 0)] + bias) + acc[(r0, 1)]


def _pmix_kernel(first, *refs):
    if first:
        xf_ref, refs = refs[0], refs[1:]
    (xm_ref, gmix_ref, win_ref, wq_ref, dwb_ref, dbs_ref, lng_ref, lnb_ref, wpw_ref, sinks_ref,
     wao_ref, wout_ref,
     xo_ref, knew_ref, vnew_ref, unew_ref,
     ubuf, uh_s, c_s, q_s, k_s, v_s, kh_s, vh_s, a_s) = refs
    T = TILE
    i = pl.program_id(0)
    slot = i % 2
    dyn_zero = jnp.minimum(i, 0)

    @pl.when(i == 0)
    def _():
        uh_s[0] = jnp.zeros((N_CCH, CONV_HIST, LANES), F32)
        kh_s[0] = jnp.zeros((N_KV, WINDOW, D_KV), BF16)
        vh_s[0] = jnp.zeros((N_KV, WINDOW, D_KV), BF16)

    ubuf[:, 0:CONV_HIST, :] = uh_s[slot]

    x = xm_ref[...]
    if first:
        x = jnp.where(i == 0, xf_ref[...], x)
    xn = _rmsnorm(x, gmix_ref[...]).astype(BF16)

    def proj(lo, hi):
        return _dot(xn, win_ref[:, lo:hi])

    rows = i * T + lax.broadcasted_iota(jnp.int32, (T, 1), 0)
    real = rows >= PAD
    lane_kh = lax.broadcasted_iota(jnp.int32, (1, D_KV), 1) // HEAD_DIM
    lanes_per_cw = CCHUNK // LANES

    def glu(cw):
        lo = cw * CCHUNK
        u = proj(O_UA + lo, O_UA + lo + CCHUNK) * _sigmoid(proj(O_UB + lo, O_UB + lo + CCHUNK))
        u = jnp.where(real, u, 0.0)
        unew_ref[:, lo:lo + CCHUNK] = u[T - CONV_HIST:, :]
        for cc in range(lanes_per_cw):
            uc = u[:, cc * LANES:(cc + 1) * LANES]
            ubuf[cw * lanes_per_cw + cc, CONV_HIST:CONV_HIST + T, :] = uc
            uh_s[1 - slot, cw * lanes_per_cw + cc] = uc[T - CONV_HIST:, :]

    def conv(cw):
        for cc in range(lanes_per_cw):
            _conv_lane_chunk(ubuf, dwb_ref, dbs_ref, c_s, cw * lanes_per_cw + cc + dyn_zero)

    def proj_q(lo, hi):
        q_s[:, lo:hi] = _dot(xn, wq_ref[:, lo:hi]).astype(BF16)

    def proj_kv():
        kv = proj(O_K, O_GC)
        k = kv[:, 0:D_KV]
        v = kv[:, D_KV:2 * D_KV]
        knew_ref[...] = k[T - WINDOW:, :]
        vnew_ref[...] = v[T - WINDOW:, :]
        for kh in range(N_KV):
            sel = lane_kh == kh
            km = jnp.where(sel, k, 0.0).astype(BF16)
            vm = jnp.where(sel, v, 0.0).astype(BF16)
            k_s[kh] = km
            v_s[kh] = vm
            kh_s[1 - slot, kh] = km[T - WINDOW:, :]
            vh_s[1 - slot, kh] = vm[T - WINDOW:, :]

    glu(0)
    conv(0)
    glu(1)
    conv(1)
    glu(2)
    conv(2)
    glu(3)
    conv(3)
    proj_q(0, 2 * D_KV)
    proj_q(2 * D_KV, 4 * D_KV)
    proj_kv()
    sgc = _sigmoid(proj(O_GC, O_GA))
    sga = _sigmoid(proj(O_GA, O_END))

    qi = lax.broadcasted_iota(jnp.int32, (ABLK, 2 * ABLK), 0)
    kj = lax.broadcasted_iota(jnp.int32, (ABLK, 2 * ABLK), 1)
    band = (kj >= qi) & (kj <= qi + WINDOW)

    def window(cur, hist, kh, r0):
        if r0 == 0:
            return jnp.concatenate([hist[slot, kh], cur[kh, 0:ABLK, :]], axis=0)
        return cur[kh, r0 - ABLK:r0 + ABLK, :]

    def attn_scores(r0):
        qb = q_s[r0:r0 + ABLK, :]
        qall = jnp.concatenate([qb[:, g * D_KV:(g + 1) * D_KV] for g in range(GROUP)], axis=0)
        return [_dot_nt(qall, window(k_s, kh_s, kh, r0)) for kh in range(N_KV)]

    def attn_finish(r0, scores):
        ok = band & (i * T + r0 - WINDOW + kj >= PAD)
        probs = [[None] * N_KV for _ in range(GROUP)]
        for kh in range(N_KV):
            s = scores[kh]
            for g in range(GROUP):
                sink = sinks_ref[g * N_KV + kh]
                sh = jnp.where(ok, s[g * ABLK:(g + 1) * ABLK, :], NEG)
                m = jnp.maximum(jnp.max(sh, axis=-1, keepdims=True), sink)
                e = jnp.exp2(sh - m)
                l = jnp.sum(e, axis=-1, keepdims=True) + jnp.exp2(sink - m)
                probs[g][kh] = (e * (1.0 / l)).astype(BF16)
        vst = jnp.concatenate([window(v_s, vh_s, kh, r0) for kh in range(N_KV)], axis=0)
        for g in range(GROUP):
            o = _dot(jnp.concatenate(probs[g], axis=1), vst)
            a_s[r0:r0 + ABLK, g * D_KV:(g + 1) * D_KV] = o.astype(BF16)

    n_blk = T // ABLK
    pw_cols = D_MODEL // n_blk
    scores = attn_scores(0)
    cfull = jnp.concatenate([c_s[c] for c in range(N_CCH)], axis=1)
    y = _layernorm(cfull, lng_ref[...], lnb_ref[...])
    y = (y * _sigmoid(y)).astype(BF16)
    cbr = []
    for b in range(n_blk):
        nxt = attn_scores((b + 1) * ABLK) if b + 1 < n_blk else None
        attn_finish(b * ABLK, scores)
        cbr.append(_dot(y, wpw_ref[:, b * pw_cols:(b + 1) * pw_cols]))
        scores = nxt
    cbr = jnp.concatenate(cbr, axis=1)
    abr = _dot(a_s[...], wao_ref[...])

    mix = sgc * cbr + sga * abr
    xo_ref[...] = x + _dot(mix.astype(BF16), wout_ref[...])


def _layer_spec(l, shape):
    zeros = (0,) * len(shape)
    return pl.BlockSpec((None,) + tuple(shape), lambda i: (l,) + zeros,
                        pipeline_mode=pl.Buffered(1))


def _const_spec(shape):
    zeros = (0,) * len(shape)
    return pl.BlockSpec(shape, lambda i: zeros, pipeline_mode=pl.Buffered(1))


def _pmix_call(l, n_tiles, x_first, x_main, gmix, win, wq, dwb, dbs, lng, lnb, wpw, sinks, wao,
               wout):
    T = TILE
    first = l == 0
    if first:
        x_spec = pl.BlockSpec((T, D_MODEL), lambda i: (jnp.maximum(i - 1, 0), 0))
    else:
        x_spec = pl.BlockSpec((T, D_MODEL), lambda i: (i, 0))
    in_specs = [
        x_spec,
        _layer_spec(l, (1, D_MODEL)),
        _layer_spec(l, (D_MODEL, O_END)),
        _layer_spec(l, (D_MODEL, D_MODEL)),
        _layer_spec(l, (N_CCH, CONV_K * SUBLANES, LANES)),
        _layer_spec(l, (N_CCH, 1, LANES)),
        _layer_spec(l, (1, D_MODEL)),
        _layer_spec(l, (1, D_MODEL)),
        _layer_spec(l, (D_MODEL, D_MODEL)),
        pl.BlockSpec(memory_space=pltpu.SMEM),
        _layer_spec(l, (D_MODEL, D_MODEL)),
        _layer_spec(l, (D_MODEL, D_MODEL)),
    ]
    args = [x_main, gmix, win, wq, dwb, dbs, lng, lnb, wpw, sinks, wao, wout]
    if first:
        in_specs = [_const_spec((T, D_MODEL))] + in_specs
        args = [x_first] + args
    out_shape = (
        jax.ShapeDtypeStruct((n_tiles * T, D_MODEL), F32),
        jax.ShapeDtypeStruct((WINDOW, D_KV), F32),
        jax.ShapeDtypeStruct((WINDOW, D_KV), F32),
        jax.ShapeDtypeStruct((CONV_HIST, D_MODEL), F32),
    )
    out_specs = (
        pl.BlockSpec((T, D_MODEL), lambda i: (i, 0)),
        pl.BlockSpec((WINDOW, D_KV), lambda i: (0, 0)),
        pl.BlockSpec((WINDOW, D_KV), lambda i: (0, 0)),
        pl.BlockSpec((CONV_HIST, D_MODEL), lambda i: (0, 0)),
    )
    scratch = [
        pltpu.VMEM((N_CCH, CONV_HIST + T, LANES), F32),
        pltpu.VMEM((2, N_CCH, CONV_HIST, LANES), F32),
        pltpu.VMEM((N_CCH, T, LANES), F32),
        pltpu.VMEM((T, D_MODEL), BF16),
        pltpu.VMEM((N_KV, T, D_KV), BF16),
        pltpu.VMEM((N_KV, T, D_KV), BF16),
        pltpu.VMEM((2, N_KV, WINDOW, D_KV), BF16),
        pltpu.VMEM((2, N_KV, WINDOW, D_KV), BF16),
        pltpu.VMEM((T, D_MODEL), BF16),
    ]
    return pl.pallas_call(
        functools.partial(_pmix_kernel, first),
        grid=(n_tiles,),
        in_specs=in_specs,
        out_specs=out_specs,
        out_shape=out_shape,
        scratch_shapes=scratch,
        compiler_params=pltpu.CompilerParams(
            dimension_semantics=("arbitrary",), vmem_limit_bytes=VMEM_LIMIT),
        name="prompt_mixer",
    )(*args)


def _pffn_kernel(final, *refs):
    if final:
        (xm_ref, gffn_ref, wup_ref, fdw_ref, fdb_ref, wdn_ref, gfin_ref,
         xo_ref, hnew_ref, hbuf) = refs
    else:
        (xm_ref, gffn_ref, wup_ref, fdw_ref, fdb_ref, wdn_ref,
         xo_ref, hnew_ref, hbuf) = refs
    T = TILE
    i = pl.program_id(0)

    @pl.when(i == 0)
    def _():
        hbuf[:, 0:FFN_HIST, :] = jnp.zeros((N_FCH, FFN_HIST, LANES), F32)

    x = xm_ref[...]
    xn = _rmsnorm(x, gffn_ref[...]).astype(BF16)
    h = _dot(xn, wup_ref[:, 0:D_FF])
    gate = _dot(xn, wup_ref[:, D_FF:2 * D_FF])
    rows = i * T + lax.broadcasted_iota(jnp.int32, (T, 1), 0)
    h = jnp.where(rows >= PAD, h, 0.0)
    hnew_ref[...] = h[T - FFN_HIST:, :]
    acts = []
    for c in range(N_FCH):
        sl = slice(c * LANES, (c + 1) * LANES)
        hc = h[:, sl]
        hbuf[c, FFN_HIST:FFN_HIST + T, :] = hc
        cc = (hbuf[c, FFN_HIST - 2:FFN_HIST - 2 + T, :] * fdw_ref[0:1, sl]
              + hbuf[c, FFN_HIST - 1:FFN_HIST - 1 + T, :] * fdw_ref[1:2, sl]
              + hc * fdw_ref[2:3, sl] + fdb_ref[:, sl])
        acts.append((_gelu_tanh(cc) * gate[:, sl]).astype(BF16))
        hbuf[c, 0:FFN_HIST, :] = hbuf[c, T:T + FFN_HIST, :]
    y = x + _dot(jnp.concatenate(acts, axis=1), wdn_ref[...])
    if final:
        y = _rmsnorm(y, gfin_ref[...])
    xo_ref[...] = y


def _pffn_call(l, final, n_tiles, xm, gffn, wup, fdw, fdb, wdn, gfin):
    T = TILE
    in_specs = [
        pl.BlockSpec((T, D_MODEL), lambda i: (i, 0)),
        _layer_spec(l, (1, D_MODEL)),
        _layer_spec(l, (D_MODEL, 2 * D_FF)),
        _layer_spec(l, (FFN_K, D_FF)),
        _layer_spec(l, (1, D_FF)),
        _layer_spec(l, (D_FF, D_MODEL)),
    ]
    args = [xm, gffn, wup, fdw, fdb, wdn]
    if final:
        in_specs.append(_const_spec((1, D_MODEL)))
        args.append(gfin)
        x_out = jax.ShapeDtypeStruct(((n_tiles - 1) * T, D_MODEL), F32)
        x_out_spec = pl.BlockSpec((T, D_MODEL), lambda i: (jnp.maximum(i - 1, 0), 0))
    else:
        x_out = jax.ShapeDtypeStruct((n_tiles * T, D_MODEL), F32)
        x_out_spec = pl.BlockSpec((T, D_MODEL), lambda i: (i, 0))
    return pl.pallas_call(
        functools.partial(_pffn_kernel, final),
        grid=(n_tiles,),
        in_specs=in_specs,
        out_specs=(x_out_spec, pl.BlockSpec((FFN_HIST, D_FF), lambda i: (0, 0))),
        out_shape=(x_out, jax.ShapeDtypeStruct((FFN_HIST, D_FF), F32)),
        scratch_shapes=[pltpu.VMEM((N_FCH, FFN_HIST + T, LANES), F32)],
        compiler_params=pltpu.CompilerParams(
            dimension_semantics=("arbitrary",), vmem_limit_bytes=VMEM_LIMIT),
        name="prompt_ffn",
    )(*args)


SQ = 4
S_STEPS = 32
N_SMIX_IN = 15


def _smix_kernel(n_alias, *refs):
    (xs_ref, gmix_ref, win_ref, wq_ref, dw_ref, db_ref, lng_ref, lnb_ref, wpw_ref,
     sinkc_ref, wao_ref, wout_ref, sc_ref, kt_ref, vt_ref) = refs[:N_SMIX_IN]
    (xo_ref, nkt_ref, nvt_ref, nct_ref,
     u_s, q_s, k_s, v_s, kvt_s, gc_s, ga_s, acc_s, a_s) = refs[N_SMIX_IN + n_alias:]
    s = pl.program_id(0)
    n_hist = CONV_K - 1

    @pl.when(s == 0)
    def _():
        xn = _rmsnorm(xs_ref[...], gmix_ref[...]).astype(BF16)

        def proj(lo, hi):
            return _dot(xn, win_ref[:, lo:hi])

        u_s[...] = proj(O_UA, O_UB) * _sigmoid(proj(O_UB, O_Q))
        q_s[...] = _dot(xn, wq_ref[...])
        kv = proj(O_K, O_GC)
        k_s[...] = kv[:, 0:D_KV]
        v_s[...] = kv[:, D_KV:2 * D_KV]
        kvt_s[...] = kv.T
        gc_s[...] = _sigmoid(proj(O_GC, O_GA))
        ga_s[...] = _sigmoid(proj(O_GA, O_END))
        acc_s[...] = jnp.zeros(acc_s.shape, F32)

    slab = sc_ref[...]

    @pl.when(s < n_hist)
    def _():
        acc_s[...] += slab * dw_ref[pl.ds(s, 1), :]

    nct_ref[...] = jnp.where(s < n_hist, slab, u_s[...])

    lane_kh = lax.broadcasted_iota(jnp.int32, (1, D_KV), 1) // HEAD_DIM
    lane_w = lax.broadcasted_iota(jnp.int32, (1, WINDOW), 1)
    sink = sinkc_ref[:, 0:1]
    for bb in range(SQ):
        b = s * SQ + bb
        qrow = q_s[pl.ds(b, 1), :]
        knew = k_s[pl.ds(b, 1), :]
        vnew = v_s[pl.ds(b, 1), :]
        kt = kt_ref[bb].reshape(D_KV, WINDOW)
        vt = vt_ref[bb].reshape(D_KV, WINDOW)
        qm = jnp.concatenate(
            [jnp.where(lane_kh == kh, qrow[:, g * D_KV:(g + 1) * D_KV], 0.0)
             for g in range(GROUP) for kh in range(N_KV)], axis=0)
        sc = _dot(qm.astype(BF16), kt.astype(BF16))
        snew = jnp.sum(qm * knew, axis=-1, keepdims=True)
        m = jnp.maximum(jnp.maximum(jnp.max(sc, axis=-1, keepdims=True), snew), sink)
        e = jnp.exp2(sc - m)
        en = jnp.exp2(snew - m)
        inv = 1.0 / (jnp.sum(e, axis=-1, keepdims=True) + en + jnp.exp2(sink - m))
        o = _dot_nt((e * inv).astype(BF16), vt.astype(BF16)) + (en * inv) * vnew
        slabs = []
        for g in range(GROUP):
            acc = None
            for kh in range(N_KV):
                r = g * N_KV + kh
                t = jnp.where(lane_kh == kh, o[r:r + 1, :], 0.0)
                acc = t if acc is None else acc + t
            slabs.append(acc)
        a_s[pl.ds(b, 1), :] = jnp.concatenate(slabs, axis=1)
        kcol = pltpu.roll(kvt_s[0:D_KV, :], WINDOW - 1 - b, axis=1)
        vcol = pltpu.roll(kvt_s[D_KV:2 * D_KV, :], WINDOW - 1 - b, axis=1)
        last = lane_w == WINDOW - 1
        nkt_ref[bb] = jnp.where(last, kcol, pltpu.roll(kt, WINDOW - 1, axis=1)).reshape(
            N_KV, HEAD_DIM, WINDOW)
        nvt_ref[bb] = jnp.where(last, vcol, pltpu.roll(vt, WINDOW - 1, axis=1)).reshape(
            N_KV, HEAD_DIM, WINDOW)

    @pl.when(s == S_STEPS - 1)
    def _():
        c = acc_s[...] + u_s[...] * dw_ref[n_hist:CONV_K, :] + db_ref[...]
        y = _layernorm(c, lng_ref[...], lnb_ref[...])
        y = y * _sigmoid(y)
        cbr = _dot(y.astype(BF16), wpw_ref[...])
        abr = _dot(a_s[...].astype(BF16), wao_ref[...])
        mix = gc_s[...] * cbr + ga_s[...] * abr
        xo_ref[...] = xs_ref[...] + _dot(mix.astype(BF16), wout_ref[...])


def _smix_call(l, xs, gmix, win, wq, dw, db, lng, lnb, wpw, sinkc, wao, wout, sct, kt, vt, prev):
    nb = xs.shape[0]
    depth = kt.shape[0]
    n_hist = CONV_K - 1
    assert nb == SQ * S_STEPS and nb == WINDOW and S_STEPS >= n_hist + 1
    kv_spec = pl.BlockSpec((None, SQ, N_KV, HEAD_DIM, WINDOW), lambda s: (l, s, 0, 0, 0))
    in_specs = [
        _const_spec((nb, D_MODEL)),
        _layer_spec(l, (1, D_MODEL)),
        _layer_spec(l, (D_MODEL, O_END)),
        _layer_spec(l, (D_MODEL, D_MODEL)),
        _layer_spec(l, (CONV_K, D_MODEL)),
        _layer_spec(l, (1, D_MODEL)),
        _layer_spec(l, (1, D_MODEL)),
        _layer_spec(l, (1, D_MODEL)),
        _layer_spec(l, (D_MODEL, D_MODEL)),
        _layer_spec(l, (GROUP * N_KV, LANES)),
        _layer_spec(l, (D_MODEL, D_MODEL)),
        _layer_spec(l, (D_MODEL, D_MODEL)),
        pl.BlockSpec((None, None, nb, D_MODEL), lambda s: (l, jnp.minimum(s, n_hist - 1), 0, 0)),
        kv_spec,
        kv_spec,
    ]
    args = [xs, gmix, win, wq, dw, db, lng, lnb, wpw, sinkc, wao, wout, sct, kt, vt]
    assert len(args) == N_SMIX_IN
    aliases = {}
    if prev is not None:
        in_specs += [pl.BlockSpec(memory_space=pl.ANY)] * len(prev)
        aliases = {len(args) + k: 1 + k for k in range(len(prev))}
        args += list(prev)
    out_shape = (
        jax.ShapeDtypeStruct((nb, D_MODEL), F32),
        jax.ShapeDtypeStruct((depth, nb, N_KV, HEAD_DIM, WINDOW), F32),
        jax.ShapeDtypeStruct((depth, nb, N_KV, HEAD_DIM, WINDOW), F32),
        jax.ShapeDtypeStruct((depth, n_hist, nb, D_MODEL), F32),
    )
    out_specs = (
        pl.BlockSpec((nb, D_MODEL), lambda s: (0, 0)),
        kv_spec,
        kv_spec,
        pl.BlockSpec((None, None, nb, D_MODEL),
                     lambda s: (l, jnp.clip(s - 1, 0, n_hist - 1), 0, 0)),
    )
    scratch = [
        pltpu.VMEM((nb, D_MODEL), F32),
        pltpu.VMEM((nb, D_MODEL), F32),
        pltpu.VMEM((nb, D_KV), F32),
        pltpu.VMEM((nb, D_KV), F32),
        pltpu.VMEM((2 * D_KV, nb), F32),
        pltpu.VMEM((nb, D_MODEL), F32),
        pltpu.VMEM((nb, D_MODEL), F32),
        pltpu.VMEM((nb, D_MODEL), F32),
        pltpu.VMEM((nb, D_MODEL), F32),
    ]
    return pl.pallas_call(
        functools.partial(_smix_kernel, 0 if prev is None else len(prev)),
        grid=(S_STEPS,),
        in_specs=in_specs,
        out_specs=out_specs,
        out_shape=out_shape,
        scratch_shapes=scratch,
        input_output_aliases=aliases,
        compiler_params=pltpu.CompilerParams(
            dimension_semantics=("arbitrary",), vmem_limit_bytes=VMEM_LIMIT),
        name="sample_mixer",
    )(*args)


def _sffn_kernel(final, *refs):
    if final:
        (x_ref, gffn_ref, wup_ref, s0_ref, s1_ref, fdw_ref, fdb_ref, wdn_ref, gfin_ref,
         xo_ref, h_ref) = refs
    else:
        (x_ref, gffn_ref, wup_ref, s0_ref, s1_ref, fdw_ref, fdb_ref, wdn_ref,
         xo_ref, h_ref) = refs
    x = x_ref[...]
    xn = _rmsnorm(x, gffn_ref[...]).astype(BF16)
    h = _dot(xn, wup_ref[:, 0:D_FF])
    gate = _dot(xn, wup_ref[:, D_FF:2 * D_FF])
    h_ref[...] = h
    cc = (s0_ref[...] * fdw_ref[0:1, :] + s1_ref[...] * fdw_ref[1:2, :]
          + h * fdw_ref[2:3, :] + fdb_ref[...])
    y = x + _dot((_gelu_tanh(cc) * gate).astype(BF16), wdn_ref[...])
    if final:
        y = _rmsnorm(y, gfin_ref[...])
    xo_ref[...] = y


def _sffn_call(l, final, x, gffn, wup, s0, s1, fdw, fdb, wdn, gfin):
    nb = x.shape[0]
    in_specs = [
        _const_spec((nb, D_MODEL)),
        _layer_spec(l, (1, D_MODEL)),
        _layer_spec(l, (D_MODEL, 2 * D_FF)),
        _const_spec((nb, D_FF)),
        _const_spec((nb, D_FF)),
        _layer_spec(l, (FFN_K, D_FF)),
        _layer_spec(l, (1, D_FF)),
        _layer_spec(l, (D_FF, D_MODEL)),
    ]
    args = [x, gffn, wup, s0, s1, fdw, fdb, wdn]
    if final:
        in_specs.append(_const_spec((1, D_MODEL)))
        args.append(gfin)
    return pl.pallas_call(
        functools.partial(_sffn_kernel, final),
        grid=(1,),
        in_specs=in_specs,
        out_specs=(pl.BlockSpec((nb, D_MODEL), lambda i: (0, 0)),
                   pl.BlockSpec((nb, D_FF), lambda i: (0, 0))),
        out_shape=(jax.ShapeDtypeStruct((nb, D_MODEL), F32),
                   jax.ShapeDtypeStruct((nb, D_FF), F32)),
        compiler_params=pltpu.CompilerParams(
            dimension_semantics=("arbitrary",), vmem_limit_bytes=VMEM_LIMIT),
        name="sample_ffn",
    )(*args)


def _group_major(w, axis):
    shape = w.shape
    w = w.reshape(shape[:axis] + (N_KV, GROUP, HEAD_DIM) + shape[axis + 1:])
    w = jnp.swapaxes(w, axis, axis + 1)
    return w.reshape(shape)


def kernel(x_prompt, x_sample, cache_swa_k, cache_swa_v, state_conv, state_ffn_conv, meta_tokens, norm_mix, w_in, conv_dw, conv_db, conv_ln_g, conv_ln_b, w_conv_pw, attn_sinks, w_attn_o, w_out, norm_ffn, w_ffn_up, ffn_dw, ffn_db, w_ffn_down, norm_final):
    depth = w_in.shape[0]
    batch, seq, _ = x_prompt.shape
    assert batch == 1 and seq % TILE == 0
    n_tiles = 1 + seq // TILE
    nb = x_sample.shape[0]

    win_b = w_in.astype(BF16)
    wq_b = (_group_major(w_in[:, :, O_Q:O_K], 2) * (ATT_SCALE * LOG2E)).astype(BF16)
    wpw_b = w_conv_pw.astype(BF16)
    wao_b = _group_major(w_attn_o, 1).astype(BF16)
    wout_b = w_out.astype(BF16)
    wup_b = w_ffn_up.astype(BF16)
    wdn_b = w_ffn_down.astype(BF16)
    sinks_gk = jnp.swapaxes(attn_sinks.reshape(depth, N_KV, GROUP), 1, 2).reshape(depth, GROUP * N_KV)
    sinks_gk = sinks_gk * LOG2E
    sinks_col = jnp.broadcast_to(sinks_gk[:, :, None], (depth, GROUP * N_KV, LANES))
    dwb = conv_dw.reshape(depth, CONV_K, N_CCH, LANES).transpose(0, 2, 1, 3)
    dwb = jnp.repeat(dwb, SUBLANES, axis=2)
    dbs = conv_db.reshape(depth, N_CCH, 1, LANES)
    row = lambda a: a[:, None, :]
    gmix, db, lng, lnb, gffn, fdb = map(row, (norm_mix, conv_db, conv_ln_g, conv_ln_b, norm_ffn, ffn_db))
    gfin = norm_final[None, :]

    x_first = jnp.concatenate([jnp.zeros((PAD, D_MODEL), F32), meta_tokens.astype(F32)], axis=0)
    xp = x_prompt[0]
    xs = x_sample[:, 0, :]
    kt_all = jnp.transpose(cache_swa_k, (0, 1, 3, 4, 2))
    vt_all = jnp.transpose(cache_swa_v, (0, 1, 3, 4, 2))
    sc_t = jnp.transpose(state_conv, (0, 2, 1, 3))
    sample_state = None

    kp, vp, cp, fp, fs = [], [], [], [], []
    for l in range(depth):
        final = l == depth - 1
        xm, knew, vnew, unew = _pmix_call(
            l, n_tiles, x_first, xp, gmix, win_b, wq_b, dwb, dbs, lng, lnb, wpw_b, sinks_gk[l],
            wao_b, wout_b)
        xp, hnew = _pffn_call(l, final, n_tiles, xm, gffn, wup_b, ffn_dw, fdb, wdn_b, gfin)
        kp.append(knew)
        vp.append(vnew)
        cp.append(unew[CONV_HIST - (CONV_K - 1):])
        fp.append(hnew[FFN_HIST - (FFN_K - 1):])

        xsm, *sample_state = _smix_call(
            l, xs, gmix, win_b, wq_b, conv_dw, db, lng, lnb, wpw_b, sinks_col, wao_b, wout_b,
            sc_t, kt_all, vt_all, sample_state)
        s0 = state_ffn_conv[l, :, 0, :]
        s1 = state_ffn_conv[l, :, 1, :]
        xs, hs = _sffn_call(l, final, xsm, gffn, wup_b, s0, s1, ffn_dw, fdb, wdn_b, gfin)
        fs.append(jnp.stack([s1, hs], axis=1))

    nkt, nvt, nct = sample_state
    kv5 = lambda a, b: jnp.stack(a).reshape(depth, b, WINDOW, N_KV, HEAD_DIM)
    return (xp[None], xs[:, None, :],
            kv5(kp, 1), kv5(vp, 1), jnp.stack(cp)[:, None], jnp.stack(fp)[:, None],
            jnp.transpose(nkt, (0, 1, 4, 2, 3)), jnp.transpose(nvt, (0, 1, 4, 2, 3)),
            jnp.transpose(nct, (0, 2, 1, 3)), jnp.stack(fs))
```

```python
import functools
import math

import jax
import jax.numpy as jnp
from jax import lax
from jax.experimental import pallas as pl
from jax.experimental.pallas import tpu as pltpu

F32 = jnp.float32
BF16 = jnp.bfloat16

D_MODEL = 1024
N_META = 16
HEAD_DIM = 64
N_KV = 4
GROUP = 4
D_KV = N_KV * HEAD_DIM
WINDOW = 128
CONV_K = 31
D_FF = 2816
FFN_K = 3
EPS = 1e-6
NEG = -1e30
ATT_SCALE = 1.0 / math.sqrt(HEAD_DIM)
LOG2E = math.log2(math.e)

LANES = 128
SUBLANES = 8
N_CCH = D_MODEL // LANES
N_FCH = D_FF // LANES

TILE = 512
PAD = TILE - N_META
CONV_HIST = 32
CCHUNK = 256
CONV_RB = 64
FFN_HIST = 8
ABLK = 128

O_UA, O_UB, O_Q, O_K, O_V, O_GC, O_GA, O_END = 0, 1024, 2048, 3072, 3328, 3584, 4608, 5632

VMEM_LIMIT = 60 * 1024 * 1024


def _sigmoid(x):
    return 1.0 / (1.0 + jnp.exp(-x))


def _gelu_tanh(x):
    return 0.5 * x * (1.0 + jnp.tanh(0.7978845608028654 * (x + 0.044715 * (x * x * x))))


def _rmsnorm(x, g):
    ms = jnp.mean(x * x, axis=-1, keepdims=True)
    return x * lax.rsqrt(ms + EPS) * g


def _layernorm(x, g, b):
    mu = jnp.mean(x, axis=-1, keepdims=True)
    d = x - mu
    var = jnp.mean(d * d, axis=-1, keepdims=True)
    return d * lax.rsqrt(var + EPS) * g + b


def _dot(a, b):
    return jnp.dot(a, b, preferred_element_type=F32)


def _dot_nt(a, b):
    return lax.dot_general(a, b, (((1,), (1,)), ((), ())), preferred_element_type=F32)


def _conv_lane_chunk(ubuf, cu, dwb_ref, dbs_ref, c_s, c):
    first_start = CONV_HIST - (CONV_K - 1)
    bias = jnp.broadcast_to(dbs_ref[c], (SUBLANES, LANES))
    for base in range(0, TILE, CONV_RB):
        groups = range(base, base + CONV_RB, SUBLANES)
        acc = {}
        for phase in range(SUBLANES):
            ws = {j: dwb_ref[c, j * SUBLANES:(j + 1) * SUBLANES, :]
                  for j in range(phase, CONV_K, SUBLANES)}
            starts = sorted({first_start + r0 + j for r0 in groups for j in ws})
            for start in starts:
                win = ubuf[cu, start:start + SUBLANES, :]
                for r0 in groups:
                    j = start - first_start - r0
                    if j in ws:
                        t = win * ws[j]
                        key = (r0, phase % 2)
                        acc[key] = t if key not in acc else acc[key] + t
        for r0 in groups:
            c_s[c, r0:r0 + SUBLANES, :] = (acc[(r0, 0)] + bias) + acc[(r0, 1)]


def _pmix_kernel(first, *refs):
    if first:
        xf_ref, refs = refs[0], refs[1:]
    (xm_ref, gmix_ref, win_ref, wq_ref, dwb_ref, dbs_ref, lng_ref, lnb_ref, wpw_ref, sinks_ref,
     wao_ref, wout_ref,
     xo_ref, knew_ref, vnew_ref, unew_ref,
     ub0, ub1, ub2, ub3, uh_s, c_s, q_s, k_s, v_s, kh_s, vh_s, a_s) = refs
    ubufs = (ub0, ub1, ub2, ub3)
    T = TILE
    i = pl.program_id(0)
    slot = i % 2
    dyn_zero = jnp.minimum(i, 0)

    @pl.when(i == 0)
    def _():
        uh_s[0] = jnp.zeros((N_CCH, CONV_HIST, LANES), F32)
        kh_s[0] = jnp.zeros((N_KV, WINDOW, D_KV), BF16)
        vh_s[0] = jnp.zeros((N_KV, WINDOW, D_KV), BF16)

    for cw, ub in enumerate(ubufs):
        ub[:, 0:CONV_HIST, :] = uh_s[slot, cw * (CCHUNK // LANES):(cw + 1) * (CCHUNK // LANES)]

    x = xm_ref[...]
    if first:
        x = jnp.where(i == 0, xf_ref[...], x)
    xn = _rmsnorm(x, gmix_ref[...]).astype(BF16)

    def proj(lo, hi):
        return _dot(xn, win_ref[:, lo:hi])

    rows = i * T + lax.broadcasted_iota(jnp.int32, (T, 1), 0)
    real = rows >= PAD
    lane_kh = lax.broadcasted_iota(jnp.int32, (1, D_KV), 1) // HEAD_DIM
    lanes_per_cw = CCHUNK // LANES

    def glu(cw):
        lo = cw * CCHUNK
        u = proj(O_UA + lo, O_UA + lo + CCHUNK) * _sigmoid(proj(O_UB + lo, O_UB + lo + CCHUNK))
        u = jnp.where(real, u, 0.0)
        unew_ref[:, lo:lo + CCHUNK] = u[T - CONV_HIST:, :]
        for cc in range(lanes_per_cw):
            uc = u[:, cc * LANES:(cc + 1) * LANES]
            ubufs[cw][cc, CONV_HIST:CONV_HIST + T, :] = uc
            uh_s[1 - slot, cw * lanes_per_cw + cc] = uc[T - CONV_HIST:, :]

    def conv(cw):
        for cc in range(lanes_per_cw):
            _conv_lane_chunk(ubufs[cw], cc + dyn_zero, dwb_ref, dbs_ref, c_s,
                             cw * lanes_per_cw + cc + dyn_zero)

    def proj_q(lo, hi):
        q_s[:, lo:hi] = _dot(xn, wq_ref[:, lo:hi]).astype(BF16)

    def proj_kv():
        kv = proj(O_K, O_GC)
        k = kv[:, 0:D_KV]
        v = kv[:, D_KV:2 * D_KV]
        knew_ref[...] = k[T - WINDOW:, :]
        vnew_ref[...] = v[T - WINDOW:, :]
        for kh in range(N_KV):
            sel = lane_kh == kh
            km = jnp.where(sel, k, 0.0).astype(BF16)
            vm = jnp.where(sel, v, 0.0).astype(BF16)
            k_s[kh] = km
            v_s[kh] = vm
            kh_s[1 - slot, kh] = km[T - WINDOW:, :]
            vh_s[1 - slot, kh] = vm[T - WINDOW:, :]

    glu(0)
    conv(0)
    glu(1)
    conv(1)
    glu(2)
    conv(2)
    glu(3)
    conv(3)
    proj_q(0, 2 * D_KV)
    proj_q(2 * D_KV, 4 * D_KV)
    proj_kv()
    sgc = _sigmoid(proj(O_GC, O_GA))
    sga = _sigmoid(proj(O_GA, O_END))

    qi = lax.broadcasted_iota(jnp.int32, (ABLK, 2 * ABLK), 0)
    kj = lax.broadcasted_iota(jnp.int32, (ABLK, 2 * ABLK), 1)
    band = (kj >= qi) & (kj <= qi + WINDOW)

    def window(cur, hist, kh, r0):
        if r0 == 0:
            return jnp.concatenate([hist[slot, kh], cur[kh, 0:ABLK, :]], axis=0)
        return cur[kh, r0 - ABLK:r0 + ABLK, :]

    def attn_scores(r0):
        qb = q_s[r0:r0 + ABLK, :]
        qall = jnp.concatenate([qb[:, g * D_KV:(g + 1) * D_KV] for g in range(GROUP)], axis=0)
        return [_dot_nt(qall, window(k_s, kh_s, kh, r0)) for kh in range(N_KV)]

    def attn_finish(r0, scores):
        ok = band & (i * T + r0 - WINDOW + kj >= PAD)
        probs = [[None] * N_KV for _ in range(GROUP)]
        for kh in range(N_KV):
            s = scores[kh]
            for g in range(GROUP):
                sink = sinks_ref[g * N_KV + kh]
                sh = jnp.where(ok, s[g * ABLK:(g + 1) * ABLK, :], NEG)
                m = jnp.maximum(jnp.max(sh, axis=-1, keepdims=True), sink)
                e = jnp.exp2(sh - m)
                l = jnp.sum(e, axis=-1, keepdims=True) + jnp.exp2(sink - m)
                probs[g][kh] = (e * (1.0 / l)).astype(BF16)
        vst = jnp.concatenate([window(v_s, vh_s, kh, r0) for kh in range(N_KV)], axis=0)
        for g in range(GROUP):
            o = _dot(jnp.concatenate(probs[g], axis=1), vst)
            a_s[r0:r0 + ABLK, g * D_KV:(g + 1) * D_KV] = o.astype(BF16)

    n_blk = T // ABLK
    pw_cols = D_MODEL // n_blk
    scores = attn_scores(0)
    cfull = jnp.concatenate([c_s[c] for c in range(N_CCH)], axis=1)
    y = _layernorm(cfull, lng_ref[...], lnb_ref[...])
    y = (y * _sigmoid(y)).astype(BF16)
    cbr = []
    for b in range(n_blk):
        nxt = attn_scores((b + 1) * ABLK) if b + 1 < n_blk else None
        attn_finish(b * ABLK, scores)
        cbr.append(_dot(y, wpw_ref[:, b * pw_cols:(b + 1) * pw_cols]))
        scores = nxt
    cbr = jnp.concatenate(cbr, axis=1)
    abr = _dot(a_s[...], wao_ref[...])

    mix = sgc * cbr + sga * abr
    xo_ref[...] = x + _dot(mix.astype(BF16), wout_ref[...])


def _layer_spec(l, shape):
    zeros = (0,) * len(shape)
    return pl.BlockSpec((None,) + tuple(shape), lambda i: (l,) + zeros,
                        pipeline_mode=pl.Buffered(1))


def _const_spec(shape):
    zeros = (0,) * len(shape)
    return pl.BlockSpec(shape, lambda i: zeros, pipeline_mode=pl.Buffered(1))


def _pmix_call(l, n_tiles, x_first, x_main, gmix, win, wq, dwb, dbs, lng, lnb, wpw, sinks, wao,
               wout):
    T = TILE
    first = l == 0
    if first:
        x_spec = pl.BlockSpec((T, D_MODEL), lambda i: (jnp.maximum(i - 1, 0), 0))
    else:
        x_spec = pl.BlockSpec((T, D_MODEL), lambda i: (i, 0))
    in_specs = [
        x_spec,
        _layer_spec(l, (1, D_MODEL)),
        _layer_spec(l, (D_MODEL, O_END)),
        _layer_spec(l, (D_MODEL, D_MODEL)),
        _layer_spec(l, (N_CCH, CONV_K * SUBLANES, LANES)),
        _layer_spec(l, (N_CCH, 1, LANES)),
        _layer_spec(l, (1, D_MODEL)),
        _layer_spec(l, (1, D_MODEL)),
        _layer_spec(l, (D_MODEL, D_MODEL)),
        pl.BlockSpec(memory_space=pltpu.SMEM),
        _layer_spec(l, (D_MODEL, D_MODEL)),
        _layer_spec(l, (D_MODEL, D_MODEL)),
    ]
    args = [x_main, gmix, win, wq, dwb, dbs, lng, lnb, wpw, sinks, wao, wout]
    if first:
        in_specs = [_const_spec((T, D_MODEL))] + in_specs
        args = [x_first] + args
    out_shape = (
        jax.ShapeDtypeStruct((n_tiles * T, D_MODEL), F32),
        jax.ShapeDtypeStruct((WINDOW, D_KV), F32),
        jax.ShapeDtypeStruct((WINDOW, D_KV), F32),
        jax.ShapeDtypeStruct((CONV_HIST, D_MODEL), F32),
    )
    out_specs = (
        pl.BlockSpec((T, D_MODEL), lambda i: (i, 0)),
        pl.BlockSpec((WINDOW, D_KV), lambda i: (0, 0)),
        pl.BlockSpec((WINDOW, D_KV), lambda i: (0, 0)),
        pl.BlockSpec((CONV_HIST, D_MODEL), lambda i: (0, 0)),
    )
    scratch = [
        pltpu.VMEM((CCHUNK // LANES, CONV_HIST + T, LANES), F32),
        pltpu.VMEM((CCHUNK // LANES, CONV_HIST + T, LANES), F32),
        pltpu.VMEM((CCHUNK // LANES, CONV_HIST + T, LANES), F32),
        pltpu.VMEM((CCHUNK // LANES, CONV_HIST + T, LANES), F32),
        pltpu.VMEM((2, N_CCH, CONV_HIST, LANES), F32),
        pltpu.VMEM((N_CCH, T, LANES), F32),
        pltpu.VMEM((T, D_MODEL), BF16),
        pltpu.VMEM((N_KV, T, D_KV), BF16),
        pltpu.VMEM((N_KV, T, D_KV), BF16),
        pltpu.VMEM((2, N_KV, WINDOW, D_KV), BF16),
        pltpu.VMEM((2, N_KV, WINDOW, D_KV), BF16),
        pltpu.VMEM((T, D_MODEL), BF16),
    ]
    return pl.pallas_call(
        functools.partial(_pmix_kernel, first),
        grid=(n_tiles,),
        in_specs=in_specs,
        out_specs=out_specs,
        out_shape=out_shape,
        scratch_shapes=scratch,
        compiler_params=pltpu.CompilerParams(
            dimension_semantics=("arbitrary",), vmem_limit_bytes=VMEM_LIMIT),
        name="prompt_mixer",
    )(*args)


def _pffn_kernel(final, *refs):
    if final:
        (xm_ref, gffn_ref, wup_ref, fdw_ref, fdb_ref, wdn_ref, gfin_ref,
         xo_ref, hnew_ref, hbuf) = refs
    else:
        (xm_ref, gffn_ref, wup_ref, fdw_ref, fdb_ref, wdn_ref,
         xo_ref, hnew_ref, hbuf) = refs
    T = TILE
    i = pl.program_id(0)

    @pl.when(i == 0)
    def _():
        hbuf[:, 0:FFN_HIST, :] = jnp.zeros((N_FCH, FFN_HIST, LANES), F32)

    x = xm_ref[...]
    xn = _rmsnorm(x, gffn_ref[...]).astype(BF16)
    h = _dot(xn, wup_ref[:, 0:D_FF])
    gate = _dot(xn, wup_ref[:, D_FF:2 * D_FF])
    rows = i * T + lax.broadcasted_iota(jnp.int32, (T, 1), 0)
    h = jnp.where(rows >= PAD, h, 0.0)
    hnew_ref[...] = h[T - FFN_HIST:, :]
    acts = []
    for c in range(N_FCH):
        sl = slice(c * LANES, (c + 1) * LANES)
        hc = h[:, sl]
        hbuf[c, FFN_HIST:FFN_HIST + T, :] = hc
        cc = (hbuf[c, FFN_HIST - 2:FFN_HIST - 2 + T, :] * fdw_ref[0:1, sl]
              + hbuf[c, FFN_HIST - 1:FFN_HIST - 1 + T, :] * fdw_ref[1:2, sl]
              + hc * fdw_ref[2:3, sl] + fdb_ref[:, sl])
        acts.append((_gelu_tanh(cc) * gate[:, sl]).astype(BF16))
        hbuf[c, 0:FFN_HIST, :] = hbuf[c, T:T + FFN_HIST, :]
    y = x + _dot(jnp.concatenate(acts, axis=1), wdn_ref[...])
    if final:
        y = _rmsnorm(y, gfin_ref[...])
    xo_ref[...] = y


def _pffn_call(l, final, n_tiles, xm, gffn, wup, fdw, fdb, wdn, gfin):
    T = TILE
    in_specs = [
        pl.BlockSpec((T, D_MODEL), lambda i: (i, 0)),
        _layer_spec(l, (1, D_MODEL)),
        _layer_spec(l, (D_MODEL, 2 * D_FF)),
        _layer_spec(l, (FFN_K, D_FF)),
        _layer_spec(l, (1, D_FF)),
        _layer_spec(l, (D_FF, D_MODEL)),
    ]
    args = [xm, gffn, wup, fdw, fdb, wdn]
    if final:
        in_specs.append(_const_spec((1, D_MODEL)))
        args.append(gfin)
        x_out = jax.ShapeDtypeStruct(((n_tiles - 1) * T, D_MODEL), F32)
        x_out_spec = pl.BlockSpec((T, D_MODEL), lambda i: (jnp.maximum(i - 1, 0), 0))
    else:
        x_out = jax.ShapeDtypeStruct((n_tiles * T, D_MODEL), F32)
        x_out_spec = pl.BlockSpec((T, D_MODEL), lambda i: (i, 0))
    return pl.pallas_call(
        functools.partial(_pffn_kernel, final),
        grid=(n_tiles,),
        in_specs=in_specs,
        out_specs=(x_out_spec, pl.BlockSpec((FFN_HIST, D_FF), lambda i: (0, 0))),
        out_shape=(x_out, jax.ShapeDtypeStruct((FFN_HIST, D_FF), F32)),
        scratch_shapes=[pltpu.VMEM((N_FCH, FFN_HIST + T, LANES), F32)],
        compiler_params=pltpu.CompilerParams(
            dimension_semantics=("arbitrary",), vmem_limit_bytes=VMEM_LIMIT),
        name="prompt_ffn",
    )(*args)


SQ = 4
S_STEPS = 32
N_SMIX_IN = 15


def _smix_kernel(n_alias, *refs):
    (xs_ref, gmix_ref, win_ref, wq_ref, dw_ref, db_ref, lng_ref, lnb_ref, wpw_ref,
     sinkc_ref, wao_ref, wout_ref, sc_ref, kt_ref, vt_ref) = refs[:N_SMIX_IN]
    (xo_ref, nkt_ref, nvt_ref, nct_ref,
     u_s, q_s, k_s, v_s, kvt_s, gc_s, ga_s, acc_s, a_s) = refs[N_SMIX_IN + n_alias:]
    s = pl.program_id(0)
    n_hist = CONV_K - 1

    @pl.when(s == 0)
    def _():
        xn = _rmsnorm(xs_ref[...], gmix_ref[...]).astype(BF16)

        def proj(lo, hi):
            return _dot(xn, win_ref[:, lo:hi])

        u_s[...] = proj(O_UA, O_UB) * _sigmoid(proj(O_UB, O_Q))
        q_s[...] = _dot(xn, wq_ref[...])
        kv = proj(O_K, O_GC)
        k_s[...] = kv[:, 0:D_KV]
        v_s[...] = kv[:, D_KV:2 * D_KV]
        kvt_s[...] = kv.T
        gc_s[...] = _sigmoid(proj(O_GC, O_GA))
        ga_s[...] = _sigmoid(proj(O_GA, O_END))
        acc_s[...] = jnp.zeros(acc_s.shape, F32)

    slab = sc_ref[...]

    @pl.when(s < n_hist)
    def _():
        acc_s[...] += slab * dw_ref[pl.ds(s, 1), :]

    nct_ref[...] = jnp.where(s < n_hist, slab, u_s[...])

    lane_kh = lax.broadcasted_iota(jnp.int32, (1, D_KV), 1) // HEAD_DIM
    lane_w = lax.broadcasted_iota(jnp.int32, (1, WINDOW), 1)
    sink = sinkc_ref[:, 0:1]
    for bb in range(SQ):
        b = s * SQ + bb
        qrow = q_s[pl.ds(b, 1), :]
        knew = k_s[pl.ds(b, 1), :]
        vnew = v_s[pl.ds(b, 1), :]
        kt = kt_ref[bb].reshape(D_KV, WINDOW)
        vt = vt_ref[bb].reshape(D_KV, WINDOW)
        qm = jnp.concatenate(
            [jnp.where(lane_kh == kh, qrow[:, g * D_KV:(g + 1) * D_KV], 0.0)
             for g in range(GROUP) for kh in range(N_KV)], axis=0)
        sc = _dot(qm.astype(BF16), kt.astype(BF16))
        snew = jnp.sum(qm * knew, axis=-1, keepdims=True)
        m = jnp.maximum(jnp.maximum(jnp.max(sc, axis=-1, keepdims=True), snew), sink)
        e = jnp.exp2(sc - m)
        en = jnp.exp2(snew - m)
        inv = 1.0 / (jnp.sum(e, axis=-1, keepdims=True) + en + jnp.exp2(sink - m))
        o = _dot_nt((e * inv).astype(BF16), vt.astype(BF16)) + (en * inv) * vnew
        slabs = []
        for g in range(GROUP):
            acc = None
            for kh in range(N_KV):
                r = g * N_KV + kh
                t = jnp.where(lane_kh == kh, o[r:r + 1, :], 0.0)
                acc = t if acc is None else acc + t
            slabs.append(acc)
        a_s[pl.ds(b, 1), :] = jnp.concatenate(slabs, axis=1)
        kcol = pltpu.roll(kvt_s[0:D_KV, :], WINDOW - 1 - b, axis=1)
        vcol = pltpu.roll(kvt_s[D_KV:2 * D_KV, :], WINDOW - 1 - b, axis=1)
        last = lane_w == WINDOW - 1
        nkt_ref[bb] = jnp.where(last, kcol, pltpu.roll(kt, WINDOW - 1, axis=1)).reshape(
            N_KV, HEAD_DIM, WINDOW)
        nvt_ref[bb] = jnp.where(last, vcol, pltpu.roll(vt, WINDOW - 1, axis=1)).reshape(
            N_KV, HEAD_DIM, WINDOW)

    @pl.when(s == S_STEPS - 1)
    def _():
        c = acc_s[...] + u_s[...] * dw_ref[n_hist:CONV_K, :] + db_ref[...]
        y = _layernorm(c, lng_ref[...], lnb_ref[...])
        y = y * _sigmoid(y)
        cbr = _dot(y.astype(BF16), wpw_ref[...])
        abr = _dot(a_s[...].astype(BF16), wao_ref[...])
        mix = gc_s[...] * cbr + ga_s[...] * abr
        xo_ref[...] = xs_ref[...] + _dot(mix.astype(BF16), wout_ref[...])


def _smix_call(l, xs, gmix, win, wq, dw, db, lng, lnb, wpw, sinkc, wao, wout, sct, kt, vt, prev):
    nb = xs.shape[0]
    depth = kt.shape[0]
    n_hist = CONV_K - 1
    assert nb == SQ * S_STEPS and nb == WINDOW and S_STEPS >= n_hist + 1
    kv_spec = pl.BlockSpec((None, SQ, N_KV, HEAD_DIM, WINDOW), lambda s: (l, s, 0, 0, 0))
    in_specs = [
        _const_spec((nb, D_MODEL)),
        _layer_spec(l, (1, D_MODEL)),
        _layer_spec(l, (D_MODEL, O_END)),
        _layer_spec(l, (D_MODEL, D_MODEL)),
        _layer_spec(l, (CONV_K, D_MODEL)),
        _layer_spec(l, (1, D_MODEL)),
        _layer_spec(l, (1, D_MODEL)),
        _layer_spec(l, (1, D_MODEL)),
        _layer_spec(l, (D_MODEL, D_MODEL)),
        _layer_spec(l, (GROUP * N_KV, LANES)),
        _layer_spec(l, (D_MODEL, D_MODEL)),
        _layer_spec(l, (D_MODEL, D_MODEL)),
        pl.BlockSpec((None, None, nb, D_MODEL), lambda s: (l, jnp.minimum(s, n_hist - 1), 0, 0)),
        kv_spec,
        kv_spec,
    ]
    args = [xs, gmix, win, wq, dw, db, lng, lnb, wpw, sinkc, wao, wout, sct, kt, vt]
    assert len(args) == N_SMIX_IN
    aliases = {}
    if prev is not None:
        in_specs += [pl.BlockSpec(memory_space=pl.ANY)] * len(prev)
        aliases = {len(args) + k: 1 + k for k in range(len(prev))}
        args += list(prev)
    out_shape = (
        jax.ShapeDtypeStruct((nb, D_MODEL), F32),
        jax.ShapeDtypeStruct((depth, nb, N_KV, HEAD_DIM, WINDOW), F32),
        jax.ShapeDtypeStruct((depth, nb, N_KV, HEAD_DIM, WINDOW), F32),
        jax.ShapeDtypeStruct((depth, n_hist, nb, D_MODEL), F32),
    )
    out_specs = (
        pl.BlockSpec((nb, D_MODEL), lambda s: (0, 0)),
        kv_spec,
        kv_spec,
        pl.BlockSpec((None, None, nb, D_MODEL),
                     lambda s: (l, jnp.clip(s - 1, 0, n_hist - 1), 0, 0)),
    )
    scratch = [
        pltpu.VMEM((nb, D_MODEL), F32),
        pltpu.VMEM((nb, D_MODEL), F32),
        pltpu.VMEM((nb, D_KV), F32),
        pltpu.VMEM((nb, D_KV), F32),
        pltpu.VMEM((2 * D_KV, nb), F32),
        pltpu.VMEM((nb, D_MODEL), F32),
        pltpu.VMEM((nb, D_MODEL), F32),
        pltpu.VMEM((nb, D_MODEL), F32),
        pltpu.VMEM((nb, D_MODEL), F32),
    ]
    return pl.pallas_call(
        functools.partial(_smix_kernel, 0 if prev is None else len(prev)),
        grid=(S_STEPS,),
        in_specs=in_specs,
        out_specs=out_specs,
        out_shape=out_shape,
        scratch_shapes=scratch,
        input_output_aliases=aliases,
        compiler_params=pltpu.CompilerParams(
            dimension_semantics=("arbitrary",), vmem_limit_bytes=VMEM_LIMIT),
        name="sample_mixer",
    )(*args)


def _sffn_kernel(final, *refs):
    if final:
        (x_ref, gffn_ref, wup_ref, s0_ref, s1_ref, fdw_ref, fdb_ref, wdn_ref, gfin_ref,
         xo_ref, h_ref) = refs
    else:
        (x_ref, gffn_ref, wup_ref, s0_ref, s1_ref, fdw_ref, fdb_ref, wdn_ref,
         xo_ref, h_ref) = refs
    x = x_ref[...]
    xn = _rmsnorm(x, gffn_ref[...]).astype(BF16)
    h = _dot(xn, wup_ref[:, 0:D_FF])
    gate = _dot(xn, wup_ref[:, D_FF:2 * D_FF])
    h_ref[...] = h
    cc = (s0_ref[...] * fdw_ref[0:1, :] + s1_ref[...] * fdw_ref[1:2, :]
          + h * fdw_ref[2:3, :] + fdb_ref[...])
    y = x + _dot((_gelu_tanh(cc) * gate).astype(BF16), wdn_ref[...])
    if final:
        y = _rmsnorm(y, gfin_ref[...])
    xo_ref[...] = y


def _sffn_call(l, final, x, gffn, wup, s0, s1, fdw, fdb, wdn, gfin):
    nb = x.shape[0]
    in_specs = [
        _const_spec((nb, D_MODEL)),
        _layer_spec(l, (1, D_MODEL)),
        _layer_spec(l, (D_MODEL, 2 * D_FF)),
        _const_spec((nb, D_FF)),
        _const_spec((nb, D_FF)),
        _layer_spec(l, (FFN_K, D_FF)),
        _layer_spec(l, (1, D_FF)),
        _layer_spec(l, (D_FF, D_MODEL)),
    ]
    args = [x, gffn, wup, s0, s1, fdw, fdb, wdn]
    if final:
        in_specs.append(_const_spec((1, D_MODEL)))
        args.append(gfin)
    return pl.pallas_call(
        functools.partial(_sffn_kernel, final),
        grid=(1,),
        in_specs=in_specs,
        out_specs=(pl.BlockSpec((nb, D_MODEL), lambda i: (0, 0)),
                   pl.BlockSpec((nb, D_FF), lambda i: (0, 0))),
        out_shape=(jax.ShapeDtypeStruct((nb, D_MODEL), F32),
                   jax.ShapeDtypeStruct((nb, D_FF), F32)),
        compiler_params=pltpu.CompilerParams(
            dimension_semantics=("arbitrary",), vmem_limit_bytes=VMEM_LIMIT),
        name="sample_ffn",
    )(*args)


def _group_major(w, axis):
    shape = w.shape
    w = w.reshape(shape[:axis] + (N_KV, GROUP, HEAD_DIM) + shape[axis + 1:])
    w = jnp.swapaxes(w, axis, axis + 1)
    return w.reshape(shape)


def kernel(x_prompt, x_sample, cache_swa_k, cache_swa_v, state_conv, state_ffn_conv, meta_tokens, norm_mix, w_in, conv_dw, conv_db, conv_ln_g, conv_ln_b, w_conv_pw, attn_sinks, w_attn_o, w_out, norm_ffn, w_ffn_up, ffn_dw, ffn_db, w_ffn_down, norm_final):
    depth = w_in.shape[0]
    batch, seq, _ = x_prompt.shape
    assert batch == 1 and seq % TILE == 0
    n_tiles = 1 + seq // TILE
    nb = x_sample.shape[0]

    win_b = w_in.astype(BF16)
    wq_b = (_group_major(w_in[:, :, O_Q:O_K], 2) * (ATT_SCALE * LOG2E)).astype(BF16)
    wpw_b = w_conv_pw.astype(BF16)
    wao_b = _group_major(w_attn_o, 1).astype(BF16)
    wout_b = w_out.astype(BF16)
    wup_b = w_ffn_up.astype(BF16)
    wdn_b = w_ffn_down.astype(BF16)
    sinks_gk = jnp.swapaxes(attn_sinks.reshape(depth, N_KV, GROUP), 1, 2).reshape(depth, GROUP * N_KV)
    sinks_gk = sinks_gk * LOG2E
    sinks_col = jnp.broadcast_to(sinks_gk[:, :, None], (depth, GROUP * N_KV, LANES))
    dwb = conv_dw.reshape(depth, CONV_K, N_CCH, LANES).transpose(0, 2, 1, 3)
    dwb = jnp.repeat(dwb, SUBLANES, axis=2)
    dbs = conv_db.reshape(depth, N_CCH, 1, LANES)
    row = lambda a: a[:, None, :]
    gmix, db, lng, lnb, gffn, fdb = map(row, (norm_mix, conv_db, conv_ln_g, conv_ln_b, norm_ffn, ffn_db))
    gfin = norm_final[None, :]

    x_first = jnp.concatenate([jnp.zeros((PAD, D_MODEL), F32), meta_tokens.astype(F32)], axis=0)
    xp = x_prompt[0]
    xs = x_sample[:, 0, :]
    kt_all = jnp.transpose(cache_swa_k, (0, 1, 3, 4, 2))
    vt_all = jnp.transpose(cache_swa_v, (0, 1, 3, 4, 2))
    sc_t = jnp.transpose(state_conv, (0, 2, 1, 3))
    sample_state = None

    kp, vp, cp, fp, fs = [], [], [], [], []
    for l in range(depth):
        final = l == depth - 1
        xm, knew, vnew, unew = _pmix_call(
            l, n_tiles, x_first, xp, gmix, win_b, wq_b, dwb, dbs, lng, lnb, wpw_b, sinks_gk[l],
            wao_b, wout_b)
        xp, hnew = _pffn_call(l, final, n_tiles, xm, gffn, wup_b, ffn_dw, fdb, wdn_b, gfin)
        kp.append(knew)
        vp.append(vnew)
        cp.append(unew[CONV_HIST - (CONV_K - 1):])
        fp.append(hnew[FFN_HIST - (FFN_K - 1):])

        xsm, *sample_state = _smix_call(
            l, xs, gmix, win_b, wq_b, conv_dw, db, lng, lnb, wpw_b, sinks_col, wao_b, wout_b,
            sc_t, kt_all, vt_all, sample_state)
        s0 = state_ffn_conv[l, :, 0, :]
        s1 = state_ffn_conv[l, :, 1, :]
        xs, hs = _sffn_call(l, final, xsm, gffn, wup_b, s0, s1, ffn_dw, fdb, wdn_b, gfin)
        fs.append(jnp.stack([s1, hs], axis=1))

    nkt, nvt, nct = sample_state
    kv5 = lambda a, b: jnp.stack(a).reshape(depth, b, WINDOW, N_KV, HEAD_DIM)
    return (xp[None], xs[:, None, :],
            kv5(kp, 1), kv5(vp, 1), jnp.stack(cp)[:, None], jnp.stack(fp)[:, None],
            jnp.transpose(nkt, (0, 1, 4, 2, 3)), jnp.transpose(nvt, (0, 1, 4, 2, 3)),
            jnp.transpose(nct, (0, 2, 1, 3)), jnp.stack(fs))
```
